```python
import jax, jax.numpy as jnp
from jax import lax
import numpy as np

D_MODEL = 1024
BATCH = 16
SEQ = 2048
DEPTH = 2
DEC_BATCH = 32
DEC_SEQ = 4
PAST_LEN = 16384
PAGE_SIZE = 128

N_MIXERS = 2
N_SB_LAYERS = (DEPTH + 1) // 2
N_ML_LAYERS = DEPTH // 2
SB_HEADS = 16
SB_HEAD_DIM = D_MODEL // SB_HEADS
SB_Q_BLOCK = 128
SB_BIAS_NEAR = -2.0
SB_BIAS_FAR = -10.0
ML_HEADS = 8
ML_V_DIM = D_MODEL // ML_HEADS
ML_QK_DIM = ML_V_DIM // 2
ML_CHUNK = 128
ML_GATE_CAP = 15.0
ML_PROJ = 2 * ML_HEADS * ML_QK_DIM + 2 * D_MODEL + 2 * ML_HEADS
FFN_HIDDEN = ((8 * D_MODEL + 3 * 256 - 1) // (3 * 256)) * 256
RMS_EPS = 1e-6

kernel_name = 'stickbreak_mlstm_hybrid_step'


def rmsnorm(x, gain):
    xf = x.astype(jnp.float32)
    y = xf * lax.rsqrt(jnp.mean(xf * xf, axis=-1, keepdims=True) + RMS_EPS)
    return (y * gain.astype(jnp.float32)).astype(x.dtype)


def swiglu_ffn(x, gain, w_gate_up, w_down):
    h = rmsnorm(x, gain)
    gate, up = jnp.split(h @ w_gate_up, 2, axis=-1)
    return (jax.nn.silu(gate) * up) @ w_down


def sb_qkv(x, gain, w_in, q_gain, k_gain):
    B, T, _ = x.shape
    h = rmsnorm(x, gain)
    qkv = (h @ w_in).reshape(B, T, 3, SB_HEADS, SB_HEAD_DIM)
    q = rmsnorm(qkv[:, :, 0], q_gain)
    k = rmsnorm(qkv[:, :, 1], k_gain)
    v = qkv[:, :, 2]
    return q, k, v


def sb_weights(z, valid, bias):
    z = z + bias.astype(jnp.float32)[None, :, None, None]
    log_beta = jax.nn.log_sigmoid(z)
    log_keep = jnp.where(valid, jax.nn.log_sigmoid(-z), 0.0)
    tail = lax.cumsum(log_keep, axis=z.ndim - 1, reverse=True) - log_keep
    return jnp.where(valid, jnp.exp(log_beta + tail), 0.0)


def sb_prompt(q, k, v, bias):
    B, T, H, Dh = q.shape
    nb = T // SB_Q_BLOCK
    scale = SB_HEAD_DIM ** -0.5
    qb = q.reshape(B, nb, SB_Q_BLOCK, H, Dh).transpose(1, 0, 2, 3, 4)
    starts = jnp.arange(nb, dtype=jnp.int32) * SB_Q_BLOCK
    key_pos = jnp.arange(T, dtype=jnp.int32)

    def block(args):
        q_blk, start = args
        q_pos = start + jnp.arange(SB_Q_BLOCK, dtype=jnp.int32)
        z = jnp.einsum('bqhd,bkhd->bhqk', q_blk, k).astype(jnp.float32) * scale
        valid = key_pos[None, :] < q_pos[:, None]
        a = sb_weights(z, valid, bias)
        return jnp.einsum('bhqk,bkhd->bqhd', a.astype(v.dtype), v)

    out = lax.map(block, (qb, starts))
    return out.transpose(1, 0, 2, 3, 4).reshape(B, T, H * Dh)


def sb_sample(q, k_new, v_new, k_past, v_past, bias):
    B, S, H, Dh = q.shape
    P = k_past.shape[1]
    scale = SB_HEAD_DIM ** -0.5
    z_past = jnp.einsum('bqhd,bkhd->bhqk', q, k_past).astype(jnp.float32) * scale
    z_new = jnp.einsum('bqhd,bkhd->bhqk', q, k_new).astype(jnp.float32) * scale
    z = jnp.concatenate([z_past, z_new], axis=-1)
    idx = jnp.arange(S)
    valid = jnp.concatenate([jnp.ones((S, P), dtype=bool), idx[None, :] < idx[:, None]], axis=-1)
    a = sb_weights(z, valid, bias).astype(v_new.dtype)
    out = (jnp.einsum('bhqk,bkhd->bqhd', a[..., :P], v_past)
           + jnp.einsum('bhqk,bkhd->bqhd', a[..., P:], v_new))
    return out.reshape(B, S, H * Dh)


def ml_project(x, gain, w_in, gate_bias):
    B, T, _ = x.shape
    h = rmsnorm(x, gain)
    qk_w = ML_HEADS * ML_QK_DIM
    q, k, v, og, gates = jnp.split(h @ w_in, [qk_w, 2 * qk_w, 2 * qk_w + D_MODEL, 2 * qk_w + 2 * D_MODEL], axis=-1)
    q = q.reshape(B, T, ML_HEADS, ML_QK_DIM).transpose(0, 2, 1, 3).astype(jnp.float32)
    k = k.reshape(B, T, ML_HEADS, ML_QK_DIM).transpose(0, 2, 1, 3).astype(jnp.float32) * (ML_QK_DIM ** -0.5)
    v = v.reshape(B, T, ML_HEADS, ML_V_DIM).transpose(0, 2, 1, 3).astype(jnp.float32)
    g = gates.astype(jnp.float32) + gate_bias.astype(jnp.float32)
    g = ML_GATE_CAP * jnp.tanh(g / ML_GATE_CAP)
    log_i = g[..., :ML_HEADS].transpose(0, 2, 1)
    log_f = jax.nn.log_sigmoid(g[..., ML_HEADS:]).transpose(0, 2, 1)
    return q, k, v, log_i, log_f, og


def mlstm_chunk(carry, inp):
    C, n, m = carry
    q, k, v, log_i, log_f = inp
    L = q.shape[2]
    b = lax.cumsum(log_f, axis=2)
    idx = jnp.arange(L)
    causal = idx[None, :] <= idx[:, None]
    D = jnp.where(causal, b[..., :, None] - b[..., None, :] + log_i[..., None, :], -jnp.inf)
    m_inter = b + m[..., None]
    m_t = jnp.maximum(m_inter, jnp.max(D, axis=-1))
    inter = jnp.exp(m_inter - m_t)
    s = jnp.einsum('bhtd,bhsd->bhts', q, k) * jnp.exp(D - m_t[..., None])
    num = inter[..., None] * jnp.einsum('bhtd,bhde->bhte', q, C) + jnp.einsum('bhts,bhse->bhte', s, v)
    den = inter * jnp.einsum('bhtd,bhd->bht', q, n) + jnp.sum(s, axis=-1)
    h = num / jnp.maximum(jnp.abs(den), jnp.exp(-m_t))[..., None]
    m_new = m_t[..., -1]
    decay = jnp.exp(b[..., -1] + m - m_new)
    w = jnp.exp(b[..., -1:] - b + log_i - m_new[..., None])
    C_new = decay[..., None, None] * C + jnp.einsum('bhs,bhsd,bhse->bhde', w, k, v)
    n_new = decay[..., None] * n + jnp.einsum('bhs,bhsd->bhd', w, k)
    return (C_new, n_new, m_new), h


def ml_prompt(q, k, v, log_i, log_f):
    B, H, T, _ = q.shape
    nc = T // ML_CHUNK

    def to_chunks(a):
        return jnp.moveaxis(a.reshape(a.shape[:2] + (nc, ML_CHUNK) + a.shape[3:]), 2, 0)

    carry0 = (jnp.zeros((B, H, ML_QK_DIM, ML_V_DIM), jnp.float32),
              jnp.zeros((B, H, ML_QK_DIM), jnp.float32),
              jnp.zeros((B, H), jnp.float32))
    carry, hs = lax.scan(mlstm_chunk, carry0, (to_chunks(q), to_chunks(k), to_chunks(v), to_chunks(log_i), to_chunks(log_f)))
    h = jnp.moveaxis(hs, 0, 2).reshape(B, H, T, ML_V_DIM)
    return h, carry


def ml_output(h, og, head_gain, w_out):
    B, H, T, DV = h.shape
    hn = h * lax.rsqrt(jnp.mean(h * h, axis=-1, keepdims=True) + RMS_EPS) * head_gain.astype(jnp.float32)[:, None, :]
    hn = hn.transpose(0, 2, 1, 3).reshape(B, T, H * DV).astype(og.dtype)
    return (hn * jax.nn.sigmoid(og)) @ w_out


def setup_inputs(seed: int = 0) -> dict:
    key = jax.random.key(seed)
    ks = jax.random.split(key, 24)
    n_pages = PAST_LEN // PAGE_SIZE
    n_phys = (DEC_BATCH * n_pages * 5) // 4

    def w(k, shape, fan_in):
        return jax.random.normal(k, shape, jnp.float32) * (fan_in ** -0.5)

    def gains(k, shape):
        return 1.0 + 0.02 * jax.random.normal(k, shape, jnp.float32)

    perm = jax.random.permutation(ks[7], n_phys).astype(jnp.int32)
    page_table = perm[:DEC_BATCH * n_pages].reshape(DEC_BATCH, n_pages)
    gate_bias = jnp.concatenate([
        0.1 * jax.random.normal(ks[15], (N_ML_LAYERS, ML_HEADS), jnp.float32),
        3.0 + 0.1 * jax.random.normal(ks[16], (N_ML_LAYERS, ML_HEADS), jnp.float32)], axis=-1)
    sb_bias = (jnp.linspace(SB_BIAS_NEAR, SB_BIAS_FAR, SB_HEADS, dtype=jnp.float32)[None, :]
               + 0.1 * jax.random.normal(ks[21], (N_SB_LAYERS, SB_HEADS), jnp.float32))
    return {
        'x_prompt': jax.random.normal(ks[0], (BATCH, SEQ, D_MODEL), jnp.float32),
        'x_sample': jax.random.normal(ks[1], (DEC_BATCH, DEC_SEQ, D_MODEL), jnp.float32),
        'cache_k': jax.random.normal(ks[2], (N_SB_LAYERS, n_phys, PAGE_SIZE, SB_HEADS, SB_HEAD_DIM), jnp.float32),
        'cache_v': jax.random.normal(ks[3], (N_SB_LAYERS, n_phys, PAGE_SIZE, SB_HEADS, SB_HEAD_DIM), jnp.float32),
        'state_C': 0.5 * jax.random.normal(ks[4], (N_ML_LAYERS, DEC_BATCH, ML_HEADS, ML_QK_DIM, ML_V_DIM), jnp.float32),
        'state_n': jax.random.normal(ks[5], (N_ML_LAYERS, DEC_BATCH, ML_HEADS, ML_QK_DIM), jnp.float32),
        'state_m': jax.random.normal(ks[6], (N_ML_LAYERS, DEC_BATCH, ML_HEADS), jnp.float32),
        'page_table': page_table,
        'norm_mix': gains(ks[8], (DEPTH, D_MODEL)),
        'norm_ffn': gains(ks[9], (DEPTH, D_MODEL)),
        'w_sb_in': w(ks[10], (N_SB_LAYERS, D_MODEL, 3 * D_MODEL), D_MODEL),
        'sb_q_gain': gains(ks[11], (N_SB_LAYERS, SB_HEAD_DIM)),
        'sb_k_gain': gains(ks[12], (N_SB_LAYERS, SB_HEAD_DIM)),
        'sb_logit_bias': sb_bias,
        'w_sb_out': w(ks[13], (N_SB_LAYERS, D_MODEL, D_MODEL), D_MODEL),
        'w_ml_in': w(ks[14], (N_ML_LAYERS, D_MODEL, ML_PROJ), D_MODEL),
        'ml_gate_bias': gate_bias,
        'ml_head_gain': gains(ks[17], (N_ML_LAYERS, ML_HEADS, ML_V_DIM)),
        'w_ml_out': w(ks[18], (N_ML_LAYERS, D_MODEL, D_MODEL), D_MODEL),
        'ffn_w_gate_up': w(ks[19], (DEPTH, D_MODEL, 2 * FFN_HIDDEN), D_MODEL),
        'ffn_w_down': w(ks[20], (DEPTH, FFN_HIDDEN, D_MODEL), FFN_HIDDEN),
    }


def reference(x_prompt, x_sample, cache_k, cache_v, state_C, state_n, state_m, page_table,
              norm_mix, norm_ffn, w_sb_in, sb_q_gain, sb_k_gain, sb_logit_bias, w_sb_out,
              w_ml_in, ml_gate_bias, ml_head_gain, w_ml_out, ffn_w_gate_up, ffn_w_down):
    xp, xs = x_prompt, x_sample
    Bs = xs.shape[0]
    past_len = page_table.shape[1] * cache_k.shape[2]
    kp_rows, vp_rows, ks_rows, vs_rows = [], [], [], []
    Cp, np_, mp, Cs, ns, ms = [], [], [], [], [], []
    for i in range(DEPTH):
        j = i // N_MIXERS
        if i % N_MIXERS == 0:
            q, k, v = sb_qkv(xp, norm_mix[i], w_sb_in[j], sb_q_gain[j], sb_k_gain[j])
            xp = xp + sb_prompt(q, k, v, sb_logit_bias[j]) @ w_sb_out[j]
            kp_rows.append(k)
            vp_rows.append(v)
            q, k, v = sb_qkv(xs, norm_mix[i], w_sb_in[j], sb_q_gain[j], sb_k_gain[j])
            k_past = cache_k[j][page_table].reshape(Bs, past_len, SB_HEADS, SB_HEAD_DIM)
            v_past = cache_v[j][page_table].reshape(Bs, past_len, SB_HEADS, SB_HEAD_DIM)
            xs = xs + sb_sample(q, k, v, k_past.astype(k.dtype), v_past.astype(v.dtype), sb_logit_bias[j]) @ w_sb_out[j]
            ks_rows.append(k)
            vs_rows.append(v)
        else:
            q, k, v, li, lf, og = ml_project(xp, norm_mix[i], w_ml_in[j], ml_gate_bias[j])
            h, (C1, n1, m1) = ml_prompt(q, k, v, li, lf)
            xp = xp + ml_output(h, og, ml_head_gain[j], w_ml_out[j])
            Cp.append(C1)
            np_.append(n1)
            mp.append(m1)
            q, k, v, li, lf, og = ml_project(xs, norm_mix[i], w_ml_in[j], ml_gate_bias[j])
            carry0 = (state_C[j].astype(jnp.float32), state_n[j].astype(jnp.float32), state_m[j].astype(jnp.float32))
            (C2, n2, m2), h = mlstm_chunk(carry0, (q, k, v, li, lf))
            xs = xs + ml_output(h, og, ml_head_gain[j], w_ml_out[j])
            Cs.append(C2)
            ns.append(n2)
            ms.append(m2)
        xp = xp + swiglu_ffn(xp, norm_ffn[i], ffn_w_gate_up[i], ffn_w_down[i])
        xs = xs + swiglu_ffn(xs, norm_ffn[i], ffn_w_gate_up[i], ffn_w_down[i])
    k_prompt = jnp.stack(kp_rows)
    v_prompt = jnp.stack(vp_rows)
    C_prompt = jnp.stack(Cp)
    n_prompt = jnp.stack(np_)
    m_prompt = jnp.stack(mp)
    k_sample = jnp.stack(ks_rows)
    v_sample = jnp.stack(vs_rows)
    C_sample = jnp.stack(Cs)
    n_sample = jnp.stack(ns)
    m_sample = jnp.stack(ms)
    return (xp, xs, k_prompt, v_prompt, C_prompt, n_prompt, m_prompt, k_sample, v_sample, C_sample, n_sample, m_sample)
```

```python
import functools

import jax
import jax.numpy as jnp
from jax import lax
from jax.experimental import pallas as pl
from jax.experimental.pallas import tpu as pltpu

D_MODEL = 1024
SB_HEADS = 16
SB_HEAD_DIM = 64
ML_HEADS = 8
ML_V_DIM = 128
ML_QK_DIM = 64
ML_CHUNK = 128
ML_GATE_CAP = 15.0
RMS_EPS = 1e-6
PAGE_SIZE = 128

LANES = 128
NORM_GROUP = 256
VMEM_LIMIT = 56 * 1024 * 1024

BF16 = jnp.bfloat16
F32 = jnp.float32


def _params(*sem):
    return pltpu.CompilerParams(dimension_semantics=sem, vmem_limit_bytes=VMEM_LIMIT)


def _resident(shape):
    nd = len(shape)
    return pl.BlockSpec(shape, lambda *_: (0,) * nd, pipeline_mode=pl.Buffered(1))


def _dot(a, b):
    return jnp.dot(a, b, preferred_element_type=F32)


def _dot_nt(a, b):
    return lax.dot_general(a, b, (((1,), (1,)), ((), ())), preferred_element_type=F32)


def _dot_tn(a, b):
    return lax.dot_general(a, b, (((0,), (0,)), ((), ())), preferred_element_type=F32)


def _rms_rows(x, gain_row):
    ms = jnp.mean(x * x, axis=-1, keepdims=True)
    return x * lax.rsqrt(ms + RMS_EPS) * gain_row


def _log_sigmoid_parts(z):
    lp = jnp.log(1.0 + jnp.exp(-jnp.abs(z)))
    lb = jnp.minimum(z, 0.0) - lp
    return lb, lb - z


def _split_bf16(x):
    hi = x.astype(BF16)
    lo = (x - hi.astype(F32)).astype(BF16)
    return hi, lo


def _sb_qkv_body(x_ref, g_ref, w_ref, qg_ref, kg_ref, q_ref, k_ref, v_ref, kb_ref, vb_ref,
                 h_ref):
    h_ref[...] = _rms_rows(x_ref[...], g_ref[...]).astype(BF16)
    r = lax.broadcasted_iota(jnp.int32, (NORM_GROUP, NORM_GROUP), 0) // SB_HEAD_DIM
    c = lax.broadcasted_iota(jnp.int32, (NORM_GROUP, NORM_GROUP), 1) // SB_HEAD_DIM
    same_head = jnp.where(r == c, 1.0, 0.0).astype(BF16)
    scale = SB_HEAD_DIM ** -0.5
    for ci in range(D_MODEL // NORM_GROUP):
        lo, hi = ci * NORM_GROUP, (ci + 1) * NORM_GROUP
        yq = _dot(h_ref[...], w_ref[:, lo:hi])
        ss = _dot((yq * yq).astype(BF16), same_head)
        qn = yq * lax.rsqrt(ss * (1.0 / SB_HEAD_DIM) + RMS_EPS) * qg_ref[:, lo:hi]
        q_ref[:, lo:hi] = (qn * scale).astype(BF16)
        yk = _dot(h_ref[...], w_ref[:, D_MODEL + lo:D_MODEL + hi])
        ss = _dot((yk * yk).astype(BF16), same_head)
        kn = yk * lax.rsqrt(ss * (1.0 / SB_HEAD_DIM) + RMS_EPS) * kg_ref[:, lo:hi]
        k_ref[:, lo:hi] = kn
        kb_ref[:, lo:hi] = kn.astype(BF16)
        yv = _dot(h_ref[...], w_ref[:, 2 * D_MODEL + lo:2 * D_MODEL + hi])
        v_ref[:, lo:hi] = yv
        vb_ref[:, lo:hi] = yv.astype(BF16)


def _sb_qkv(x, gain, w, qg, kg, tm):
    m = x.shape[0]
    row = lambda i: (i, 0)
    return pl.pallas_call(
        _sb_qkv_body,
        grid=(m // tm,),
        in_specs=[pl.BlockSpec((tm, D_MODEL), row),
                  _resident((1, D_MODEL)),
                  _resident((D_MODEL, 3 * D_MODEL)),
                  _resident((1, D_MODEL)),
                  _resident((1, D_MODEL))],
        out_specs=[pl.BlockSpec((tm, D_MODEL), row)] * 5,
        out_shape=[jax.ShapeDtypeStruct((m, D_MODEL), BF16),
                   jax.ShapeDtypeStruct((m, D_MODEL), F32),
                   jax.ShapeDtypeStruct((m, D_MODEL), F32),
                   jax.ShapeDtypeStruct((m, D_MODEL), BF16),
                   jax.ShapeDtypeStruct((m, D_MODEL), BF16)],
        scratch_shapes=[pltpu.VMEM((tm, D_MODEL), BF16)],
        compiler_params=_params("arbitrary"),
        name="sb_qkv",
    )(x, gain, w, qg, kg)


def _sb_block(qm, kb, vb, bias, carry, upper, valid):
    z = _dot_nt(qm, kb) + bias
    lb, lk = _log_sigmoid_parts(z)
    if valid is not None:
        lk = jnp.where(valid, lk, 0.0)
    hi, lo = _split_bf16(lk)
    tail = _dot(hi, upper) + _dot(lo, upper) + carry
    a = jnp.exp(lb + tail)
    if valid is not None:
        a = jnp.where(valid, a, 0.0)
    pv = _dot(a.astype(BF16), vb)
    return pv, tail[:, :1] + lk[:, :1]


def _sb_attn_body(bias_ref, q_ref, k_ref, v_ref, o_ref, *, blk):
    hp = pl.program_id(1)
    qi = pl.program_id(2)
    lane = lax.broadcasted_iota(jnp.int32, (1, LANES), 1)
    first = lane < SB_HEAD_DIM
    q = q_ref[...]
    qm = (jnp.where(first, q, jnp.zeros_like(q)), jnp.where(first, jnp.zeros_like(q), q))
    bias = (bias_ref[2 * hp], bias_ref[2 * hp + 1])
    r = lax.broadcasted_iota(jnp.int32, (blk, blk), 0)
    c = lax.broadcasted_iota(jnp.int32, (blk, blk), 1)
    upper = jnp.where(r > c, 1.0, 0.0).astype(BF16)
    valid = c < r

    def tile(j, state, mask):
        acc, c0, c1 = state
        off = pl.multiple_of(j * blk, blk)
        kb = k_ref[pl.ds(off, blk), :]
        vb = v_ref[pl.ds(off, blk), :]
        p0, c0 = _sb_block(qm[0], kb, vb, bias[0], c0, upper, mask)
        p1, c1 = _sb_block(qm[1], kb, vb, bias[1], c1, upper, mask)
        return acc + jnp.where(first, p0, p1), c0, c1

    zero = jnp.zeros((blk, 1), F32)
    state = tile(qi, (jnp.zeros((blk, LANES), F32), zero, zero), valid)
    state = lax.fori_loop(0, qi, lambda t, s: tile(qi - 1 - t, s, None), state)
    o_ref[...] = state[0].astype(o_ref.dtype)


def _sb_attn(q, kb, vb, bias, batch, seq, blk):
    m = q.shape[0]
    nq = seq // blk
    pairs = SB_HEADS // 2
    return pl.pallas_call(
        functools.partial(_sb_attn_body, blk=blk),
        grid=(batch, pairs, nq),
        in_specs=[pl.BlockSpec(memory_space=pltpu.SMEM),
                  pl.BlockSpec((blk, LANES), lambda b, p, i: (b * nq + i, p)),
                  pl.BlockSpec((seq, LANES), lambda b, p, i: (b, p)),
                  pl.BlockSpec((seq, LANES), lambda b, p, i: (b, p))],
        out_specs=pl.BlockSpec((blk, LANES), lambda b, p, i: (b * nq + i, p)),
        out_shape=jax.ShapeDtypeStruct((m, D_MODEL), BF16),
        compiler_params=_params("arbitrary", "arbitrary", "arbitrary"),
        name="sb_attn_prompt",
    )(bias, q, kb, vb)


def _sb_sample_body(pt_ref, bias_ref, q_ref, kn_ref, vn_ref, *refs, pages_per_step):
    k_refs = refs[:pages_per_step]
    v_refs = refs[pages_per_step:2 * pages_per_step]
    o_ref = refs[2 * pages_per_step]
    qbd_ref, acc_ref, carry_ref, kpad_ref, vpad_ref = refs[2 * pages_per_step + 1:]
    s = pl.program_id(1)
    nq = q_ref.shape[1]
    rows = nq * SB_HEADS
    row = lax.broadcasted_iota(jnp.int32, (rows, D_MODEL), 0)
    col = lax.broadcasted_iota(jnp.int32, (rows, D_MODEL), 1)
    own_head = (col // SB_HEAD_DIM) == (row % SB_HEADS)
    r = lax.broadcasted_iota(jnp.int32, (PAGE_SIZE, PAGE_SIZE), 0)
    c = lax.broadcasted_iota(jnp.int32, (PAGE_SIZE, PAGE_SIZE), 1)
    upper = jnp.where(r > c, 1.0, 0.0).astype(BF16)

    def visit(kb, vb, valid):
        z = _dot_nt(qbd_ref[...], kb) + bias_ref[...]
        lb, lk = _log_sigmoid_parts(z)
        if valid is not None:
            lk = jnp.where(valid, lk, 0.0)
        hi, lo = _split_bf16(lk)
        tail = _dot(hi, upper) + _dot(lo, upper) + carry_ref[...]
        a = jnp.exp(lb + tail)
        if valid is not None:
            a = jnp.where(valid, a, 0.0)
        acc_ref[...] += _dot(a.astype(BF16), vb)
        carry_ref[...] = tail[:, :1] + lk[:, :1]

    @pl.when(s == 0)
    def _():
        q = q_ref[0]
        qrep = jnp.concatenate(
            [jnp.broadcast_to(q[i:i + 1, :], (SB_HEADS, D_MODEL)) for i in range(nq)], axis=0)
        qbd_ref[...] = jnp.where(own_head, qrep, 0.0).astype(BF16)
        acc_ref[...] = jnp.zeros_like(acc_ref)
        carry_ref[...] = jnp.zeros_like(carry_ref)
        kpad_ref[...] = jnp.zeros_like(kpad_ref)
        vpad_ref[...] = jnp.zeros_like(vpad_ref)
        kpad_ref[0:nq, :] = kn_ref[0]
        vpad_ref[0:nq, :] = vn_ref[0]
        key = lax.broadcasted_iota(jnp.int32, (rows, PAGE_SIZE), 1)
        qry = lax.broadcasted_iota(jnp.int32, (rows, PAGE_SIZE), 0) // SB_HEADS
        visit(kpad_ref[...].astype(BF16), vpad_ref[...].astype(BF16), key < qry)

    @pl.when(s > 0)
    def _():
        for k_ref, v_ref in zip(k_refs, v_refs):
            visit(k_ref[0].astype(BF16), v_ref[0].astype(BF16), None)

    @pl.when(s == pl.num_programs(1) - 1)
    def _():
        own = jnp.where(own_head, acc_ref[...], 0.0)
        o_ref[0] = jnp.sum(own.reshape(nq, SB_HEADS, D_MODEL), axis=1)


def _sb_sample(q, k_new, v_new, cache_k, cache_v, page_table, bias_col, pages_per_step):
    batch, nq, _ = q.shape
    n_pages = page_table.shape[1]
    steps = n_pages // pages_per_step
    rows = nq * SB_HEADS

    def page_spec(i):
        def index(b, s, pt):
            first = jnp.maximum(s - 1, 0) * pages_per_step
            return (pt[b, n_pages - 1 - (first + i)], 0, 0)
        return pl.BlockSpec((1, PAGE_SIZE, D_MODEL), index)

    tok = pl.BlockSpec((1, nq, D_MODEL), lambda b, s, pt: (b, 0, 0))
    page_specs = [page_spec(i) for i in range(pages_per_step)]
    grid_spec = pltpu.PrefetchScalarGridSpec(
        num_scalar_prefetch=1,
        grid=(batch, steps + 1),
        in_specs=[pl.BlockSpec((rows, 1), lambda b, s, pt: (0, 0)), tok, tok, tok]
        + page_specs + page_specs,
        out_specs=tok,
        scratch_shapes=[pltpu.VMEM((rows, D_MODEL), BF16),
                        pltpu.VMEM((rows, D_MODEL), F32),
                        pltpu.VMEM((rows, 1), F32),
                        pltpu.VMEM((PAGE_SIZE, D_MODEL), F32),
                        pltpu.VMEM((PAGE_SIZE, D_MODEL), F32)],
    )
    return pl.pallas_call(
        functools.partial(_sb_sample_body, pages_per_step=pages_per_step),
        grid_spec=grid_spec,
        out_shape=jax.ShapeDtypeStruct((batch, nq, D_MODEL), F32),
        compiler_params=_params("arbitrary", "arbitrary"),
        name="sb_attn_sample",
    )(page_table, bias_col, q, k_new, v_new,
      *([cache_k] * pages_per_step), *([cache_v] * pages_per_step))


def _proj_res_body(a_ref, w_ref, x_ref, o_ref):
    o_ref[...] = x_ref[...] + _dot(a_ref[...].astype(BF16), w_ref[...])


def _proj_res(a, w, x, tm):
    m = x.shape[0]
    row = lambda i: (i, 0)
    return pl.pallas_call(
        _proj_res_body,
        grid=(m // tm,),
        in_specs=[pl.BlockSpec((tm, D_MODEL), row),
                  _resident((D_MODEL, D_MODEL)),
                  pl.BlockSpec((tm, D_MODEL), row)],
        out_specs=pl.BlockSpec((tm, D_MODEL), row),
        out_shape=jax.ShapeDtypeStruct((m, D_MODEL), F32),
        compiler_params=_params("arbitrary"),
        name="proj_res",
    )(a, w, x)


def _ffn_body(x_ref, g_ref, wgu_ref, wd_ref, o_ref, h_ref, *, hidden, chunk):
    x = x_ref[...]
    h_ref[...] = _rms_rows(x, g_ref[...]).astype(BF16)
    o_ref[...] = x

    def step(ci, _):
        off = pl.multiple_of(ci * chunk, chunk)
        off_up = pl.multiple_of(hidden + ci * chunk, chunk)
        gate = _dot(h_ref[...], wgu_ref[:, pl.ds(off, chunk)])
        up = _dot(h_ref[...], wgu_ref[:, pl.ds(off_up, chunk)])
        act = (gate * jax.nn.sigmoid(gate) * up).astype(BF16)
        o_ref[...] += _dot(act, wd_ref[pl.ds(off, chunk), :])
        return 0

    lax.fori_loop(0, hidden // chunk, step, 0)


def _ffn(x, gain, wgu, wd, tm, chunk=256):
    m = x.shape[0]
    hidden = wd.shape[0]
    row = lambda i: (i, 0)
    return pl.pallas_call(
        functools.partial(_ffn_body, hidden=hidden, chunk=chunk),
        grid=(m // tm,),
        in_specs=[pl.BlockSpec((tm, D_MODEL), row),
                  _resident((1, D_MODEL)),
                  _resident((D_MODEL, 2 * hidden)),
                  _resident((hidden, D_MODEL))],
        out_specs=pl.BlockSpec((tm, D_MODEL), row),
        out_shape=jax.ShapeDtypeStruct((m, D_MODEL), F32),
        scratch_shapes=[pltpu.VMEM((tm, D_MODEL), BF16)],
        compiler_params=_params("arbitrary"),
        name="ffn",
    )(x, gain, wgu, wd)


ML_QK_W = ML_HEADS * ML_QK_DIM
ML_GATES = 2 * ML_HEADS


def _ml_proj_body(x_ref, g_ref, w_ref, wgt_ref, gb_ref, q_ref, k_ref, v_ref, og_ref, gr_ref,
                  h_ref):
    h_ref[...] = _rms_rows(x_ref[...], g_ref[...]).astype(BF16)
    q_ref[...] = _dot(h_ref[...], w_ref[:, 0:ML_QK_W]).astype(BF16)
    k_ref[...] = (_dot(h_ref[...], w_ref[:, ML_QK_W:2 * ML_QK_W])
                  * (ML_QK_DIM ** -0.5)).astype(BF16)
    v_ref[...] = _dot(h_ref[...], w_ref[:, 2 * ML_QK_W:2 * ML_QK_W + D_MODEL]).astype(BF16)
    og_ref[...] = _dot(h_ref[...], w_ref[:, 2 * ML_QK_W + D_MODEL:2 * ML_QK_W + 2 * D_MODEL])
    g = _dot_nt(wgt_ref[...], h_ref[...]) + gb_ref[...]
    g = ML_GATE_CAP * jnp.tanh(g * (1.0 / ML_GATE_CAP))
    is_input_gate = (lax.broadcasted_iota(jnp.int32, g.shape, 0) % 4) < 2
    gr_ref[...] = jnp.where(is_input_gate, g, _log_sigmoid_parts(g)[0])


def _ml_proj(x, gain, w, wgt, gb, tm):
    m = x.shape[0]
    row = lambda i: (i, 0)
    return pl.pallas_call(
        _ml_proj_body,
        grid=(m // tm,),
        in_specs=[pl.BlockSpec((tm, D_MODEL), row),
                  _resident((1, D_MODEL)),
                  _resident((D_MODEL, 2 * ML_QK_W + 2 * D_MODEL)),
                  _resident((ML_GATES, D_MODEL)),
                  _resident((ML_GATES, 1))],
        out_specs=[pl.BlockSpec((tm, ML_QK_W), row),
                   pl.BlockSpec((tm, ML_QK_W), row),
                   pl.BlockSpec((tm, D_MODEL), row),
                   pl.BlockSpec((tm, D_MODEL), row),
                   pl.BlockSpec((ML_GATES, tm), lambda i: (0, i))],
        out_shape=[jax.ShapeDtypeStruct((m, ML_QK_W), BF16),
                   jax.ShapeDtypeStruct((m, ML_QK_W), BF16),
                   jax.ShapeDtypeStruct((m, D_MODEL), BF16),
                   jax.ShapeDtypeStruct((m, D_MODEL), F32),
                   jax.ShapeDtypeStruct((ML_GATES, m), F32)],
        scratch_shapes=[pltpu.VMEM((tm, D_MODEL), BF16)],
        compiler_params=_params("arbitrary"),
        name="ml_proj",
    )(x, gain, w, wgt, gb)


def _mlstm_body(q_ref, k_ref, v_ref, og_ref, gr_ref, hg_ref, c0_ref, n0_ref, m0_ref,
                y_ref, c_ref, n_ref, m_ref, *, seq, chunk):
    lane = lax.broadcasted_iota(jnp.int32, (1, LANES), 1)
    sub = lax.broadcasted_iota(jnp.int32, (LANES, 1), 0)
    head_lanes = (lane < ML_QK_DIM, lane >= ML_QK_DIM)
    r = lax.broadcasted_iota(jnp.int32, (chunk, chunk), 0)
    c = lax.broadcasted_iota(jnp.int32, (chunk, chunk), 1)
    causal = c <= r
    eye = c == r
    cum = jnp.where(r <= c, 1.0, 0.0)

    def to_col(x_row):
        return jnp.sum(jnp.where(eye, x_row, 0.0), axis=-1, keepdims=True)

    def step(ci, state):
        cmat, nrow, m0, m1 = state
        off = pl.multiple_of(ci * chunk, chunk)
        q = q_ref[pl.ds(off, chunk), :]
        k = k_ref[pl.ds(off, chunk), :]
        g = gr_ref[:, pl.ds(off, chunk)]
        b_rows = jnp.dot(g[2:4, :], cum, precision=lax.Precision.HIGHEST,
                         preferred_element_type=F32)
        cb = cmat.astype(BF16)
        d_c, d_n, decay, m_out = [], [], [], []
        for hd, m_prev in enumerate((m0, m1)):
            li_row = g[hd:hd + 1, :]
            b_row = b_rows[hd:hd + 1, :]
            b_col = to_col(b_row)
            dmat = jnp.where(causal, b_col - b_row + li_row, -jnp.inf)
            m_inter = b_col + m_prev
            m_t = jnp.maximum(m_inter, jnp.max(dmat, axis=-1, keepdims=True))
            inter = jnp.exp(m_inter - m_t)
            qm = jnp.where(head_lanes[hd], q, jnp.zeros_like(q))
            s = _dot_nt(qm, k) * jnp.exp(dmat - m_t)
            v = v_ref[pl.ds(off, chunk), hd * ML_V_DIM:(hd + 1) * ML_V_DIM]
            num = inter * _dot(qm, cb) + _dot(s.astype(BF16), v)
            qn = jnp.sum(qm.astype(F32) * nrow, axis=-1, keepdims=True)
            den = inter * qn + jnp.sum(s, axis=-1, keepdims=True)
            hh = num / jnp.maximum(jnp.abs(den), jnp.exp(-m_t))
            hn = hh * lax.rsqrt(jnp.mean(hh * hh, axis=-1, keepdims=True) + RMS_EPS)
            hn = hn * hg_ref[:, hd * ML_V_DIM:(hd + 1) * ML_V_DIM]
            og = og_ref[pl.ds(off, chunk), hd * ML_V_DIM:(hd + 1) * ML_V_DIM]
            y_ref[pl.ds(off, chunk), hd * ML_V_DIM:(hd + 1) * ML_V_DIM] = (
                hn * jax.nn.sigmoid(og)).astype(y_ref.dtype)
            m_new = m_t[chunk - 1:chunk, :]
            b_last = b_row[:, chunk - 1:chunk]
            w_col = to_col(jnp.exp(b_last - b_row + li_row - m_new))
            kw = jnp.where(head_lanes[hd], k.astype(F32) * w_col, 0.0)
            d_c.append(_dot_tn(kw.astype(BF16), v))
            d_n.append(jnp.sum(kw, axis=0, keepdims=True))
            decay.append(jnp.exp(b_last + m_prev - m_new))
            m_out.append(m_new)
        cmat = jnp.where(sub < ML_QK_DIM, decay[0], decay[1]) * cmat + d_c[0] + d_c[1]
        nrow = jnp.where(head_lanes[0], decay[0], decay[1]) * nrow + d_n[0] + d_n[1]
        return cmat, nrow, m_out[0], m_out[1]

    m_init = m0_ref[...]
    state = (c0_ref[...], n0_ref[...], m_init[:, 0:1], m_init[:, 1:2])
    cmat, nrow, m0, m1 = lax.fori_loop(0, seq // chunk, step, state)
    c_ref[...] = cmat
    n_ref[...] = nrow
    two = lax.broadcasted_iota(jnp.int32, (1, 2), 1)
    m_ref[...] = jnp.where(two == 0, m0, m1)


def _mlstm(q, k, v, og, gr, head_gain, c0, n0, m0, batch, seq, chunk):
    m = q.shape[0]
    pairs = ML_HEADS // 2
    tok = lambda b, p: (b, p)
    st = lambda b, p: (b, p, 0, 0)
    return pl.pallas_call(
        functools.partial(_mlstm_body, seq=seq, chunk=chunk),
        grid=(batch, pairs),
        in_specs=[pl.BlockSpec((seq, LANES), tok),
                  pl.BlockSpec((seq, LANES), tok),
                  pl.BlockSpec((seq, 2 * ML_V_DIM), tok),
                  pl.BlockSpec((seq, 2 * ML_V_DIM), tok),
                  pl.BlockSpec((None, None, 4, seq), lambda b, p: (p, b, 0, 0)),
                  pl.BlockSpec((1, 2 * ML_V_DIM), lambda b, p: (0, p)),
                  pl.BlockSpec((None, None, LANES, ML_V_DIM), st),
                  pl.BlockSpec((None, None, 1, LANES), st),
                  pl.BlockSpec((None, None, 1, 2), st)],
        out_specs=[pl.BlockSpec((seq, 2 * ML_V_DIM), tok),
                   pl.BlockSpec((None, None, LANES, ML_V_DIM), st),
                   pl.BlockSpec((None, None, 1, LANES), st),
                   pl.BlockSpec((None, None, 1, 2), st)],
        out_shape=[jax.ShapeDtypeStruct((m, D_MODEL), BF16),
                   jax.ShapeDtypeStruct((batch, pairs, LANES, ML_V_DIM), F32),
                   jax.ShapeDtypeStruct((batch, pairs, 1, LANES), F32),
                   jax.ShapeDtypeStruct((batch, pairs, 1, 2), F32)],
        compiler_params=_params("arbitrary", "arbitrary"),
        name="mlstm",
    )(q, k, v, og, gr, head_gain, c0, n0, m0)


SAMPLE_PAD = 16
PAGES_PER_STEP = 8


def _pad_tokens(a, batch, seq, value=0.0):
    a = a.reshape(batch, seq, a.shape[-1])
    a = jnp.pad(a, ((0, 0), (0, SAMPLE_PAD - seq), (0, 0)), constant_values=value)
    return a.reshape(batch * SAMPLE_PAD, a.shape[-1])


def kernel(x_prompt, x_sample, cache_k, cache_v, state_C, state_n, state_m, page_table,
           norm_mix, norm_ffn, w_sb_in, sb_q_gain, sb_k_gain, sb_logit_bias, w_sb_out,
           w_ml_in, ml_gate_bias, ml_head_gain, w_ml_out, ffn_w_gate_up, ffn_w_down):
    bp, tp, _ = x_prompt.shape
    bs, ts, _ = x_sample.shape
    depth = norm_mix.shape[0]
    n_phys = cache_k.shape[1]
    xp = x_prompt.reshape(bp * tp, D_MODEL)
    xs = x_sample.reshape(bs * ts, D_MODEL)
    tm_p, tm_s = 512, bs * ts
    ml_pairs = ML_HEADS // 2
    gate_perm = jnp.array([g for p in range(ml_pairs)
                           for g in (2 * p, 2 * p + 1, ML_HEADS + 2 * p, ML_HEADS + 2 * p + 1)])
    ck = cache_k.reshape(-1, PAGE_SIZE, D_MODEL)
    cv = cache_v.reshape(-1, PAGE_SIZE, D_MODEL)

    kp_rows, vp_rows, ks_rows, vs_rows = [], [], [], []
    c_p, n_p, m_p, c_s, n_s, m_s = [], [], [], [], [], []
    for i in range(depth):
        j = i // 2
        gain = norm_mix[i].reshape(1, D_MODEL)
        if i % 2 == 0:
            w_in = w_sb_in[j].astype(BF16)
            w_out = w_sb_out[j].astype(BF16)
            qg = jnp.tile(sb_q_gain[j], SB_HEADS).reshape(1, D_MODEL)
            kg = jnp.tile(sb_k_gain[j], SB_HEADS).reshape(1, D_MODEL)
            bias = sb_logit_bias[j]
            q, k32, v32, kb, vb = _sb_qkv(xp, gain, w_in, qg, kg, tm_p)
            att = _sb_attn(q, kb, vb, bias, bp, tp, 256)
            xp = _proj_res(att, w_out, xp, tm_p)
            kp_rows.append(k32.reshape(bp, tp, SB_HEADS, SB_HEAD_DIM))
            vp_rows.append(v32.reshape(bp, tp, SB_HEADS, SB_HEAD_DIM))
            q, k32, v32, _, _ = _sb_qkv(xs, gain, w_in, qg, kg, tm_s)
            bias_col = jnp.tile(bias, ts).reshape(ts * SB_HEADS, 1)
            att = _sb_sample(q.astype(F32).reshape(bs, ts, D_MODEL), k32.reshape(bs, ts, D_MODEL),
                             v32.reshape(bs, ts, D_MODEL), ck, cv,
                             page_table + j * n_phys, bias_col, PAGES_PER_STEP)
            xs = _proj_res(att.reshape(bs * ts, D_MODEL), w_out, xs, tm_s)
            ks_rows.append(k32.reshape(bs, ts, SB_HEADS, SB_HEAD_DIM))
            vs_rows.append(v32.reshape(bs, ts, SB_HEADS, SB_HEAD_DIM))
        else:
            n_main = 2 * ML_QK_W + 2 * D_MODEL
            w_main = w_ml_in[j][:, :n_main].astype(BF16)
            wgt = w_ml_in[j][:, n_main:][:, gate_perm].T.astype(BF16)
            gb = ml_gate_bias[j][gate_perm].reshape(ML_GATES, 1)
            w_out = w_ml_out[j].astype(BF16)
            hg = ml_head_gain[j].reshape(1, D_MODEL)
            q, k, v, og, gr = _ml_proj(xp, gain, w_main, wgt, gb, tm_p)
            zc = jnp.zeros((bp, ml_pairs, LANES, ML_V_DIM), F32)
            zn = jnp.zeros((bp, ml_pairs, 1, LANES), F32)
            zm = jnp.zeros((bp, ml_pairs, 1, 2), F32)
            gr = gr.reshape(ml_pairs, 4, bp, tp).transpose(0, 2, 1, 3)
            y, c1, n1, m1 = _mlstm(q, k, v, og, gr, hg, zc, zn, zm, bp, tp, ML_CHUNK)
            xp = _proj_res(y, w_out, xp, tm_p)
            c_p.append(c1.reshape(bp, ML_HEADS, ML_QK_DIM, ML_V_DIM))
            n_p.append(n1.reshape(bp, ML_HEADS, ML_QK_DIM))
            m_p.append(m1.reshape(bp, ML_HEADS))
            q, k, v, og, gr = _ml_proj(xs, gain, w_main, wgt, gb, tm_s)
            gr = gr.reshape(ML_GATES, bs, ts)
            pad_i = jnp.full((ML_GATES, bs, SAMPLE_PAD - ts), -jnp.inf, F32)
            pad_f = jnp.zeros((ML_GATES, bs, SAMPLE_PAD - ts), F32)
            is_input_gate = (jnp.arange(ML_GATES) % 4 < 2)[:, None, None]
            gr = jnp.concatenate([gr, jnp.where(is_input_gate, pad_i, pad_f)], axis=-1)
            gr = gr.reshape(ml_pairs, 4, bs, SAMPLE_PAD).transpose(0, 2, 1, 3)
            y, c2, n2, m2 = _mlstm(
                _pad_tokens(q, bs, ts), _pad_tokens(k, bs, ts), _pad_tokens(v, bs, ts),
                _pad_tokens(og, bs, ts), gr, hg,
                state_C[j].reshape(bs, ml_pairs, LANES, ML_V_DIM),
                state_n[j].reshape(bs, ml_pairs, 1, LANES),
                state_m[j].reshape(bs, ml_pairs, 1, 2), bs, SAMPLE_PAD, SAMPLE_PAD)
            y = y.reshape(bs, SAMPLE_PAD, D_MODEL)[:, :ts].reshape(bs * ts, D_MODEL)
            xs = _proj_res(y, w_out, xs, tm_s)
            c_s.append(c2.reshape(bs, ML_HEADS, ML_QK_DIM, ML_V_DIM))
            n_s.append(n2.reshape(bs, ML_HEADS, ML_QK_DIM))
            m_s.append(m2.reshape(bs, ML_HEADS))
        fg = norm_ffn[i].reshape(1, D_MODEL)
        wgu = ffn_w_gate_up[i].astype(BF16)
        wd = ffn_w_down[i].astype(BF16)
        xp = _ffn(xp, fg, wgu, wd, tm_p)
        xs = _ffn(xs, fg, wgu, wd, tm_s)

    return (xp.reshape(bp, tp, D_MODEL), xs.reshape(bs, ts, D_MODEL),
            jnp.stack(kp_rows), jnp.stack(vp_rows),
            jnp.stack(c_p), jnp.stack(n_p), jnp.stack(m_p),
            jnp.stack(ks_rows), jnp.stack(vs_rows),
            jnp.stack(c_s), jnp.stack(n_s), jnp.stack(m_s))
```

```python
import functools

import jax
import jax.numpy as jnp
from jax import lax
from jax.experimental import pallas as pl
from jax.experimental.pallas import tpu as pltpu

D_MODEL = 1024
SB_HEADS = 16
SB_HEAD_DIM = 64
ML_HEADS = 8
ML_V_DIM = 128
ML_QK_DIM = 64
ML_CHUNK = 128
ML_GATE_CAP = 15.0
RMS_EPS = 1e-6
PAGE_SIZE = 128

LANES = 128
SUBLANES = 8
NORM_GROUP = 256
VMEM_LIMIT = 56 * 1024 * 1024

BF16 = jnp.bfloat16
F32 = jnp.float32


def _params(*sem):
    return pltpu.CompilerParams(dimension_semantics=sem, vmem_limit_bytes=VMEM_LIMIT)


def _resident(shape):
    nd = len(shape)
    return pl.BlockSpec(shape, lambda *_: (0,) * nd, pipeline_mode=pl.Buffered(1))


def _dot(a, b):
    return jnp.dot(a, b, preferred_element_type=F32)


def _dot_nt(a, b):
    return lax.dot_general(a, b, (((1,), (1,)), ((), ())), preferred_element_type=F32)


def _dot_tn(a, b):
    return lax.dot_general(a, b, (((0,), (0,)), ((), ())), preferred_element_type=F32)


def _rms_rows(x, gain_row):
    ms = jnp.mean(x * x, axis=-1, keepdims=True)
    return x * lax.rsqrt(ms + RMS_EPS) * gain_row


def _log_sigmoid_parts(z):
    lp = jnp.log(1.0 + jnp.exp(-jnp.abs(z)))
    lb = jnp.minimum(z, 0.0) - lp
    return lb, lb - z


def _split_bf16(x):
    hi = x.astype(BF16)
    lo = (x - hi.astype(F32)).astype(BF16)
    return hi, lo


def _newer_keys(n):
    r = lax.broadcasted_iota(jnp.int32, (2 * n, n), 0) % n
    c = lax.broadcasted_iota(jnp.int32, (2 * n, n), 1)
    return jnp.where(r > c, 1.0, 0.0).astype(BF16)


def _stick_break(zs, lk_mask, carries, upper, chained=False):
    parts = []
    for z in zs:
        lb, lk = _log_sigmoid_parts(z)
        if lk_mask is not None:
            lk = jnp.where(lk_mask, lk, 0.0)
        parts.append((lb, lk, jnp.concatenate(_split_bf16(lk), axis=1)))
    tails = [_dot(hilo, upper) for _, _, hilo in parts]
    out = []
    for i, ((lb, lk, _), tail) in enumerate(zip(parts, tails)):
        carry = carries if chained else carries[i]
        a = jnp.exp(lb + (tail + carry))
        if lk_mask is not None:
            a = jnp.where(lk_mask, a, 0.0)
        carry = carry + (tail[:, :1] + lk[:, :1])
        if chained:
            carries = carry
        out.append((a, carry))
    return out


def _sb_qkv_body(x_ref, g_ref, wq_ref, wkt_ref, wvt_ref, qg_ref, kg_ref, q_ref, kt_ref, vt_ref,
                 h_ref):
    h_ref[...] = _rms_rows(x_ref[...], g_ref[...]).astype(BF16)
    r = lax.broadcasted_iota(jnp.int32, (NORM_GROUP, NORM_GROUP), 0) // SB_HEAD_DIM
    c = lax.broadcasted_iota(jnp.int32, (NORM_GROUP, NORM_GROUP), 1) // SB_HEAD_DIM
    same_head = jnp.where(r == c, 1.0, 0.0).astype(BF16)
    scale = SB_HEAD_DIM ** -0.5
    for ci in range(D_MODEL // NORM_GROUP):
        lo, hi = ci * NORM_GROUP, (ci + 1) * NORM_GROUP
        yq = _dot(h_ref[...], wq_ref[:, lo:hi])
        ss = _dot((yq * yq).astype(BF16), same_head)
        qn = yq * lax.rsqrt(ss * (1.0 / SB_HEAD_DIM) + RMS_EPS) * qg_ref[:, lo:hi]
        q_ref[:, lo:hi] = (qn * scale).astype(BF16)
    tm = h_ref.shape[0]
    yk = _dot_nt(wkt_ref[...], h_ref[...]).reshape(SB_HEADS, SB_HEAD_DIM, tm)
    ss = jnp.mean(yk * yk, axis=1, keepdims=True)
    kt_ref[...] = (yk * lax.rsqrt(ss + RMS_EPS) * kg_ref[...]).reshape(D_MODEL, tm)
    vt_ref[...] = _dot_nt(wvt_ref[...], h_ref[...])


def _sb_qkv(x, gain, wq, wkt, wvt, qg, kg, tm, seq):
    m = x.shape[0]
    per_seq = seq // tm
    row = lambda i: (i, 0)
    feat = lambda i: (i // per_seq, 0, i % per_seq)
    return pl.pallas_call(
        _sb_qkv_body,
        grid=(m // tm,),
        in_specs=[pl.BlockSpec((tm, D_MODEL), row),
                  _resident((1, D_MODEL)),
                  _resident((D_MODEL, D_MODEL)),
                  _resident((D_MODEL, D_MODEL)),
                  _resident((D_MODEL, D_MODEL)),
                  _resident((1, D_MODEL)),
                  _resident((1, SB_HEAD_DIM, 1))],
        out_specs=[pl.BlockSpec((tm, D_MODEL), row),
                   pl.BlockSpec((None, D_MODEL, tm), feat),
                   pl.BlockSpec((None, D_MODEL, tm), feat)],
        out_shape=[jax.ShapeDtypeStruct((m, D_MODEL), BF16),
                   jax.ShapeDtypeStruct((m // seq, D_MODEL, seq), F32),
                   jax.ShapeDtypeStruct((m // seq, D_MODEL, seq), F32)],
        scratch_shapes=[pltpu.VMEM((tm, D_MODEL), BF16)],
        compiler_params=_params("arbitrary"),
        name="sb_qkv",
    )(x, gain, wq, wkt, wvt, qg, kg)


def _sb_attn_body(bias_ref, q_ref, kt_ref, vt_ref, o_ref,
                  kb_ref, vb_ref, up_ref, qm_ref, acc_ref, carry_ref, *, seq, qblk, kblk):
    hp = pl.program_id(1)
    lane = lax.broadcasted_iota(jnp.int32, (1, LANES), 1)
    first = lane < SB_HEAD_DIM
    bias = (bias_ref[2 * hp], bias_ref[2 * hp + 1])
    kb_ref[...] = kt_ref[...].astype(BF16)
    for t in range(seq // LANES):
        vb_ref[t * LANES:(t + 1) * LANES, :] = (
            vt_ref[:, t * LANES:(t + 1) * LANES].T.astype(BF16))
    up_ref[...] = _newer_keys(kblk)
    r = lax.broadcasted_iota(jnp.int32, (qblk, kblk), 0)
    c = lax.broadcasted_iota(jnp.int32, (qblk, kblk), 1)
    ratio = qblk // kblk

    def tile(j, mask):
        off = pl.multiple_of(j * kblk, kblk)
        kb = kb_ref[:, pl.ds(off, kblk)]
        vb = vb_ref[pl.ds(off, kblk), :]
        zs = [_dot(qm_ref[hd], kb) + bias[hd] for hd in range(2)]
        (a0, c0), (a1, c1) = _stick_break(zs, mask, (carry_ref[0], carry_ref[1]), up_ref[...])
        p0 = _dot(a0.astype(BF16), vb)
        p1 = _dot(a1.astype(BF16), vb)
        acc_ref[...] += jnp.where(first, p0, p1)
        carry_ref[0] = c0
        carry_ref[1] = c1

    def query_block(qi, _):
        qoff = pl.multiple_of(qi * qblk, qblk)
        q = q_ref[pl.ds(qoff, qblk), :]
        qm_ref[0] = jnp.where(first, q, jnp.zeros_like(q))
        qm_ref[1] = jnp.where(first, jnp.zeros_like(q), q)
        acc_ref[...] = jnp.zeros_like(acc_ref)
        carry_ref[...] = jnp.zeros_like(carry_ref)
        for d in reversed(range(ratio)):
            tile(ratio * qi + d, c + d * kblk < r)

        def older_tile(t, _):
            tile(ratio * qi - 1 - t, None)
            return 0

        lax.fori_loop(0, ratio * qi, older_tile, 0)
        o_ref[pl.ds(qoff, qblk), :] = acc_ref[...].astype(o_ref.dtype)
        return 0

    lax.fori_loop(0, seq // qblk, query_block, 0)


def _sb_attn(q, kt, vt, bias, batch, seq, qblk, kblk):
    m = q.shape[0]
    pairs = SB_HEADS // 2
    return pl.pallas_call(
        functools.partial(_sb_attn_body, seq=seq, qblk=qblk, kblk=kblk),
        grid=(batch, pairs),
        in_specs=[pl.BlockSpec(memory_space=pltpu.SMEM),
                  pl.BlockSpec((seq, LANES), lambda b, p: (b, p)),
                  pl.BlockSpec((None, LANES, seq), lambda b, p: (b, p, 0)),
                  pl.BlockSpec((None, LANES, seq), lambda b, p: (b, p, 0))],
        out_specs=pl.BlockSpec((seq, LANES), lambda b, p: (b, p)),
        out_shape=jax.ShapeDtypeStruct((m, D_MODEL), BF16),
        scratch_shapes=[pltpu.VMEM((LANES, seq), BF16),
                        pltpu.VMEM((seq, LANES), BF16),
                        pltpu.VMEM((2 * kblk, kblk), BF16),
                        pltpu.VMEM((2, qblk, LANES), BF16),
                        pltpu.VMEM((qblk, LANES), F32),
                        pltpu.VMEM((2, qblk, 1), F32)],
        compiler_params=_params("arbitrary", "arbitrary"),
        name="sb_attn_prompt",
    )(bias, q, kt, vt)


def _sb_sample_body(pt_ref, bias_ref, q_ref, kn_ref, vn_ref, *refs, pages_per_step):
    k_refs = refs[:pages_per_step]
    v_refs = refs[pages_per_step:2 * pages_per_step]
    o_ref = refs[2 * pages_per_step]
    qbd_ref, acc_ref, carry_ref, kpad_ref, vpad_ref = refs[2 * pages_per_step + 1:]
    s = pl.program_id(1)
    nq = kn_ref.shape[-1]
    rows = SB_HEADS * SUBLANES
    upper = _newer_keys(PAGE_SIZE)

    def visit(pages, mask):
        zs = [_dot(qbd_ref[...], k3.astype(BF16).reshape(D_MODEL, PAGE_SIZE)) + bias_ref[...]
              for k3, _ in pages]
        res = _stick_break(zs, mask, carry_ref[...], upper, chained=True)
        acc = acc_ref[...]
        for (a, _), (_, v3) in zip(res, pages):
            a3 = a.reshape(SB_HEADS, SUBLANES, PAGE_SIZE)
            acc = acc + lax.dot_general(a3, v3, (((2,), (2,)), ((0,), (0,))),
                                        preferred_element_type=F32)
        acc_ref[...] = acc
        carry_ref[...] = res[-1][1]

    @pl.when(s == 0)
    def _():
        qrep = jnp.concatenate([q_ref[...]] * (D_MODEL // LANES), axis=1)
        row_head = lax.broadcasted_iota(jnp.int32, (rows, D_MODEL), 0) // SUBLANES
        col_head = lax.broadcasted_iota(jnp.int32, (rows, D_MODEL), 1) // SB_HEAD_DIM
        qbd_ref[...] = jnp.where(row_head == col_head, qrep, 0.0).astype(BF16)
        acc_ref[...] = jnp.zeros_like(acc_ref)
        carry_ref[...] = jnp.zeros_like(carry_ref)
        kpad_ref[...] = jnp.zeros_like(kpad_ref)
        vpad_ref[...] = jnp.zeros_like(vpad_ref)
        kpad_ref[:, :, 0:nq] = kn_ref[...]
        vpad_ref[:, :, 0:nq] = vn_ref[...]
        key = lax.broadcasted_iota(jnp.int32, (rows, PAGE_SIZE), 1)
        qry = lax.broadcasted_iota(jnp.int32, (rows, PAGE_SIZE), 0) % SUBLANES
        visit([(kpad_ref[...], vpad_ref[...])], key < qry)

    @pl.when(s > 0)
    def _():
        visit([(k_ref[...], v_ref[...]) for k_ref, v_ref in zip(k_refs, v_refs)], None)

    @pl.when(s == pl.num_programs(1) - 1)
    def _():
        o_ref[...] = acc_ref[...]


def _sb_sample(q, kn, vn, cache_k, cache_v, page_table, bias_col, pages_per_step):
    batch = q.shape[0]
    nq = kn.shape[-1]
    n_pages = page_table.shape[1]
    steps = n_pages // pages_per_step
    rows = SB_HEADS * SUBLANES

    def page_spec(i):
        def index(b, s, pt):
            first = jnp.maximum(s - 1, 0) * pages_per_step
            return (pt[b, n_pages - 1 - (first + i)], 0, 0, 0)
        return pl.BlockSpec((None, SB_HEADS, SB_HEAD_DIM, PAGE_SIZE), index)

    per_seq = lambda b, s, pt: (b, 0, 0, 0)
    new_spec = pl.BlockSpec((None, SB_HEADS, SB_HEAD_DIM, nq), per_seq)
    out_spec = pl.BlockSpec((None, SB_HEADS, SUBLANES, SB_HEAD_DIM), per_seq)
    page_specs = [page_spec(i) for i in range(pages_per_step)]
    grid_spec = pltpu.PrefetchScalarGridSpec(
        num_scalar_prefetch=1,
        grid=(batch, steps + 1),
        in_specs=[pl.BlockSpec((rows, 1), lambda b, s, pt: (0, 0)),
                  pl.BlockSpec((None, rows, LANES), lambda b, s, pt: (b, 0, 0)),
                  new_spec, new_spec]
        + page_specs + page_specs,
        out_specs=out_spec,
        scratch_shapes=[pltpu.VMEM((rows, D_MODEL), BF16),
                        pltpu.VMEM((SB_HEADS, SUBLANES, SB_HEAD_DIM), F32),
                        pltpu.VMEM((rows, 1), F32),
                        pltpu.VMEM((SB_HEADS, SB_HEAD_DIM, PAGE_SIZE), F32),
                        pltpu.VMEM((SB_HEADS, SB_HEAD_DIM, PAGE_SIZE), F32)],
    )
    return pl.pallas_call(
        functools.partial(_sb_sample_body, pages_per_step=pages_per_step),
        grid_spec=grid_spec,
        out_shape=jax.ShapeDtypeStruct((batch, SB_HEADS, SUBLANES, SB_HEAD_DIM), F32),
        compiler_params=_params("arbitrary", "arbitrary"),
        name="sb_attn_sample",
    )(page_table, bias_col, q, kn, vn,
      *([cache_k] * pages_per_step), *([cache_v] * pages_per_step))


def _proj_res_body(a_ref, w_ref, x_ref, o_ref):
    o_ref[...] = x_ref[...] + _dot(a_ref[...].astype(BF16), w_ref[...])


def _proj_res(a, w, x, tm):
    m = x.shape[0]
    row = lambda i: (i, 0)
    return pl.pallas_call(
        _proj_res_body,
        grid=(m // tm,),
        in_specs=[pl.BlockSpec((tm, D_MODEL), row),
                  _resident((D_MODEL, D_MODEL)),
                  pl.BlockSpec((tm, D_MODEL), row)],
        out_specs=pl.BlockSpec((tm, D_MODEL), row),
        out_shape=jax.ShapeDtypeStruct((m, D_MODEL), F32),
        compiler_params=_params("arbitrary"),
        name="proj_res",
    )(a, w, x)


def _ffn_body(x_ref, g_ref, wgu_ref, wd_ref, o_ref, h_ref, *, hidden, chunk):
    x = x_ref[...]
    h_ref[...] = _rms_rows(x, g_ref[...]).astype(BF16)
    o_ref[...] = x

    def step(ci, _):
        off = pl.multiple_of(ci * chunk, chunk)
        off_up = pl.multiple_of(hidden + ci * chunk, chunk)
        gate = _dot(h_ref[...], wgu_ref[:, pl.ds(off, chunk)])
        up = _dot(h_ref[...], wgu_ref[:, pl.ds(off_up, chunk)])
        act = (gate * jax.nn.sigmoid(gate) * up).astype(BF16)
        o_ref[...] += _dot(act, wd_ref[pl.ds(off, chunk), :])
        return 0

    lax.fori_loop(0, hidden // chunk, step, 0)


def _ffn(x, gain, wgu, wd, tm, chunk=256):
    m = x.shape[0]
    hidden = wd.shape[0]
    row = lambda i: (i, 0)
    return pl.pallas_call(
        functools.partial(_ffn_body, hidden=hidden, chunk=chunk),
        grid=(m // tm,),
        in_specs=[pl.BlockSpec((tm, D_MODEL), row),
                  _resident((1, D_MODEL)),
                  _resident((D_MODEL, 2 * hidden)),
                  _resident((hidden, D_MODEL))],
        out_specs=pl.BlockSpec((tm, D_MODEL), row),
        out_shape=jax.ShapeDtypeStruct((m, D_MODEL), F32),
        scratch_shapes=[pltpu.VMEM((tm, D_MODEL), BF16)],
        compiler_params=_params("arbitrary"),
        name="ffn",
    )(x, gain, wgu, wd)


ML_QK_W = ML_HEADS * ML_QK_DIM
ML_GATES = 2 * ML_HEADS


def _ml_proj_body(x_ref, g_ref, w_ref, wgt_ref, gb_ref, q_ref, k_ref, v_ref, og_ref, gr_ref,
                  h_ref):
    h_ref[...] = _rms_rows(x_ref[...], g_ref[...]).astype(BF16)
    q_ref[...] = _dot(h_ref[...], w_ref[:, 0:ML_QK_W]).astype(BF16)
    k_ref[...] = (_dot(h_ref[...], w_ref[:, ML_QK_W:2 * ML_QK_W])
                  * (ML_QK_DIM ** -0.5)).astype(BF16)
    v_ref[...] = _dot(h_ref[...], w_ref[:, 2 * ML_QK_W:2 * ML_QK_W + D_MODEL]).astype(BF16)
    og_ref[...] = _dot(h_ref[...], w_ref[:, 2 * ML_QK_W + D_MODEL:2 * ML_QK_W + 2 * D_MODEL])
    g = _dot_nt(wgt_ref[...], h_ref[...]) + gb_ref[...]
    g = ML_GATE_CAP * jnp.tanh(g * (1.0 / ML_GATE_CAP))
    is_input_gate = (lax.broadcasted_iota(jnp.int32, g.shape, 0) % 4) < 2
    gr_ref[...] = jnp.where(is_input_gate, g, _log_sigmoid_parts(g)[0])


def _ml_proj(x, gain, w, wgt, gb, tm):
    m = x.shape[0]
    row = lambda i: (i, 0)
    return pl.pallas_call(
        _ml_proj_body,
        grid=(m // tm,),
        in_specs=[pl.BlockSpec((tm, D_MODEL), row),
                  _resident((1, D_MODEL)),
                  _resident((D_MODEL, 2 * ML_QK_W + 2 * D_MODEL)),
                  _resident((ML_GATES, D_MODEL)),
                  _resident((ML_GATES, 1))],
        out_specs=[pl.BlockSpec((tm, ML_QK_W), row),
                   pl.BlockSpec((tm, ML_QK_W), row),
                   pl.BlockSpec((tm, D_MODEL), row),
                   pl.BlockSpec((tm, D_MODEL), row),
                   pl.BlockSpec((ML_GATES, tm), lambda i: (0, i))],
        out_shape=[jax.ShapeDtypeStruct((m, ML_QK_W), BF16),
                   jax.ShapeDtypeStruct((m, ML_QK_W), BF16),
                   jax.ShapeDtypeStruct((m, D_MODEL), BF16),
                   jax.ShapeDtypeStruct((m, D_MODEL), F32),
                   jax.ShapeDtypeStruct((ML_GATES, m), F32)],
        scratch_shapes=[pltpu.VMEM((tm, D_MODEL), BF16)],
        compiler_params=_params("arbitrary"),
        name="ml_proj",
    )(x, gain, w, wgt, gb)


def _mlstm_body(q_ref, k_ref, v_ref, og_ref, gr_ref, hg_ref, c0_ref, n0_ref, m0_ref,
                y_ref, c_ref, n_ref, m_ref, *, seq, chunk):
    lane = lax.broadcasted_iota(jnp.int32, (1, LANES), 1)
    sub = lax.broadcasted_iota(jnp.int32, (LANES, 1), 0)
    head_lanes = (lane < ML_QK_DIM, lane >= ML_QK_DIM)
    r = lax.broadcasted_iota(jnp.int32, (chunk, chunk), 0)
    c = lax.broadcasted_iota(jnp.int32, (chunk, chunk), 1)
    causal = c <= r
    eye = c == r
    cum = jnp.where(r <= c, 1.0, 0.0)

    def to_col(x_row):
        return jnp.sum(jnp.where(eye, x_row, 0.0), axis=-1, keepdims=True)

    def step(ci, state):
        cmat, nrow, m0, m1 = state
        off = pl.multiple_of(ci * chunk, chunk)
        q = q_ref[pl.ds(off, chunk), :]
        k = k_ref[pl.ds(off, chunk), :]
        g = gr_ref[:, pl.ds(off, chunk)]
        b_rows = jnp.dot(g[2:4, :], cum, precision=lax.Precision.HIGHEST,
                         preferred_element_type=F32)
        cb = cmat.astype(BF16)
        d_c, d_n, decay, m_out = [], [], [], []
        for hd, m_prev in enumerate((m0, m1)):
            li_row = g[hd:hd + 1, :]
            b_row = b_rows[hd:hd + 1, :]
            b_col = to_col(b_row)
            dmat = jnp.where(causal, b_col - b_row + li_row, -jnp.inf)
            m_inter = b_col + m_prev
            m_t = jnp.maximum(m_inter, jnp.max(dmat, axis=-1, keepdims=True))
            inter = jnp.exp(m_inter - m_t)
            qm = jnp.where(head_lanes[hd], q, jnp.zeros_like(q))
            s = _dot_nt(qm, k) * jnp.exp(dmat - m_t)
            v = v_ref[pl.ds(off, chunk), hd * ML_V_DIM:(hd + 1) * ML_V_DIM]
            num = inter * _dot(qm, cb) + _dot(s.astype(BF16), v)
            qn = jnp.sum(qm.astype(F32) * nrow, axis=-1, keepdims=True)
            den = inter * qn + jnp.sum(s, axis=-1, keepdims=True)
            hh = num / jnp.maximum(jnp.abs(den), jnp.exp(-m_t))
            hn = hh * lax.rsqrt(jnp.mean(hh * hh, axis=-1, keepdims=True) + RMS_EPS)
            hn = hn * hg_ref[:, hd * ML_V_DIM:(hd + 1) * ML_V_DIM]
            og = og_ref[pl.ds(off, chunk), hd * ML_V_DIM:(hd + 1) * ML_V_DIM]
            y_ref[pl.ds(off, chunk), hd * ML_V_DIM:(hd + 1) * ML_V_DIM] = (
                hn * jax.nn.sigmoid(og)).astype(y_ref.dtype)
            m_new = m_t[chunk - 1:chunk, :]
            b_last = b_row[:, chunk - 1:chunk]
            w_col = to_col(jnp.exp(b_last - b_row + li_row - m_new))
            kw = jnp.where(head_lanes[hd], k.astype(F32) * w_col, 0.0)
            d_c.append(_dot_tn(kw.astype(BF16), v))
            d_n.append(jnp.sum(kw, axis=0, keepdims=True))
            decay.append(jnp.exp(b_last + m_prev - m_new))
            m_out.append(m_new)
        cmat = jnp.where(sub < ML_QK_DIM, decay[0], decay[1]) * cmat + d_c[0] + d_c[1]
        nrow = jnp.where(head_lanes[0], decay[0], decay[1]) * nrow + d_n[0] + d_n[1]
        return cmat, nrow, m_out[0], m_out[1]

    m_init = m0_ref[...]
    state = (c0_ref[...], n0_ref[...], m_init[:, 0:1], m_init[:, 1:2])
    cmat, nrow, m0, m1 = lax.fori_loop(0, seq // chunk, step, state)
    c_ref[...] = cmat
    n_ref[...] = nrow
    two = lax.broadcasted_iota(jnp.int32, (1, 2), 1)
    m_ref[...] = jnp.where(two == 0, m0, m1)


def _mlstm(q, k, v, og, gr, head_gain, c0, n0, m0, batch, seq, chunk):
    m = q.shape[0]
    pairs = ML_HEADS // 2
    tok = lambda b, p: (b, p)
    st = lambda b, p: (b, p, 0, 0)
    return pl.pallas_call(
        functools.partial(_mlstm_body, seq=seq, chunk=chunk),
        grid=(batch, pairs),
        in_specs=[pl.BlockSpec((seq, LANES), tok),
                  pl.BlockSpec((seq, LANES), tok),
                  pl.BlockSpec((seq, 2 * ML_V_DIM), tok),
                  pl.BlockSpec((seq, 2 * ML_V_DIM), tok),
                  pl.BlockSpec((None, None, 4, seq), lambda b, p: (p, b, 0, 0)),
                  pl.BlockSpec((1, 2 * ML_V_DIM), lambda b, p: (0, p)),
                  pl.BlockSpec((None, None, LANES, ML_V_DIM), st),
                  pl.BlockSpec((None, None, 1, LANES), st),
                  pl.BlockSpec((None, None, 1, 2), st)],
        out_specs=[pl.BlockSpec((seq, 2 * ML_V_DIM), tok),
                   pl.BlockSpec((None, None, LANES, ML_V_DIM), st),
                   pl.BlockSpec((None, None, 1, LANES), st),
                   pl.BlockSpec((None, None, 1, 2), st)],
        out_shape=[jax.ShapeDtypeStruct((m, D_MODEL), BF16),
                   jax.ShapeDtypeStruct((batch, pairs, LANES, ML_V_DIM), F32),
                   jax.ShapeDtypeStruct((batch, pairs, 1, LANES), F32),
                   jax.ShapeDtypeStruct((batch, pairs, 1, 2), F32)],
        compiler_params=_params("arbitrary", "arbitrary"),
        name="mlstm",
    )(q, k, v, og, gr, head_gain, c0, n0, m0)


SAMPLE_PAD = 16
PAGES_PER_STEP = 8
SB_QUERY_BLOCK = 512
SB_KEY_BLOCK = 256


def _pad_tokens(a, batch, seq, value=0.0):
    a = a.reshape(batch, seq, a.shape[-1])
    a = jnp.pad(a, ((0, 0), (0, SAMPLE_PAD - seq), (0, 0)), constant_values=value)
    return a.reshape(batch * SAMPLE_PAD, a.shape[-1])


def _heads_last(xt, batch, seq):
    return xt.reshape(batch, SB_HEADS, SB_HEAD_DIM, seq).transpose(0, 3, 1, 2)


def kernel(x_prompt, x_sample, cache_k, cache_v, state_C, state_n, state_m, page_table,
           norm_mix, norm_ffn, w_sb_in, sb_q_gain, sb_k_gain, sb_logit_bias, w_sb_out,
           w_ml_in, ml_gate_bias, ml_head_gain, w_ml_out, ffn_w_gate_up, ffn_w_down):
    bp, tp, _ = x_prompt.shape
    bs, ts, _ = x_sample.shape
    depth = norm_mix.shape[0]
    n_phys = cache_k.shape[1]
    xp = x_prompt.reshape(bp * tp, D_MODEL)
    xs = x_sample.reshape(bs * ts, D_MODEL)
    tm_p, tm_s = 512, bs * ts
    ml_pairs = ML_HEADS // 2
    gate_perm = jnp.array([g for p in range(ml_pairs)
                           for g in (2 * p, 2 * p + 1, ML_HEADS + 2 * p, ML_HEADS + 2 * p + 1)])
    ck = cache_k.transpose(0, 1, 3, 4, 2).reshape(-1, SB_HEADS, SB_HEAD_DIM, PAGE_SIZE)
    cv = cache_v.transpose(0, 1, 3, 4, 2).reshape(-1, SB_HEADS, SB_HEAD_DIM, PAGE_SIZE)

    kp_rows, vp_rows, ks_rows, vs_rows = [], [], [], []
    c_p, n_p, m_p, c_s, n_s, m_s = [], [], [], [], [], []
    for i in range(depth):
        j = i // 2
        gain = norm_mix[i].reshape(1, D_MODEL)
        if i % 2 == 0:
            wq = w_sb_in[j][:, :D_MODEL].astype(BF16)
            wkt = w_sb_in[j][:, D_MODEL:2 * D_MODEL].T.astype(BF16)
            wvt = w_sb_in[j][:, 2 * D_MODEL:].T.astype(BF16)
            w_out = w_sb_out[j].astype(BF16)
            qg = jnp.tile(sb_q_gain[j], SB_HEADS).reshape(1, D_MODEL)
            kg = sb_k_gain[j].reshape(1, SB_HEAD_DIM, 1)
            bias = sb_logit_bias[j]
            q, kt, vt = _sb_qkv(xp, gain, wq, wkt, wvt, qg, kg, tm_p, tp)
            att = _sb_attn(q, kt, vt, bias, bp, tp, SB_QUERY_BLOCK, SB_KEY_BLOCK)
            xp = _proj_res(att, w_out, xp, tm_p)
            kp_rows.append(_heads_last(kt, bp, tp))
            vp_rows.append(_heads_last(vt, bp, tp))
            q, kt, vt = _sb_qkv(xs, gain, wq, wkt, wvt, qg, kg, tm_s, tm_s)
            q8 = q.astype(F32).reshape(bs, ts, SB_HEADS, SB_HEAD_DIM).transpose(0, 2, 1, 3)
            q8 = jnp.pad(q8, ((0, 0), (0, 0), (0, SUBLANES - ts), (0, 0)))
            q8 = jnp.concatenate([q8, q8], axis=-1).reshape(bs, SB_HEADS * SUBLANES, LANES)
            new = lambda xt: xt.reshape(SB_HEADS, SB_HEAD_DIM, bs, ts).transpose(2, 0, 1, 3)
            bias_col = jnp.repeat(bias, SUBLANES).reshape(SB_HEADS * SUBLANES, 1)
            att = _sb_sample(q8, new(kt), new(vt), ck, cv, page_table + j * n_phys,
                             bias_col, PAGES_PER_STEP)
            att = att[:, :, :ts].transpose(0, 2, 1, 3).reshape(bs * ts, D_MODEL)
            xs = _proj_res(att, w_out, xs, tm_s)
            ks_rows.append(_heads_last(kt, 1, bs * ts).reshape(bs, ts, SB_HEADS, SB_HEAD_DIM))
            vs_rows.append(_heads_last(vt, 1, bs * ts).reshape(bs, ts, SB_HEADS, SB_HEAD_DIM))
        else:
            n_main = 2 * ML_QK_W + 2 * D_MODEL
            w_main = w_ml_in[j][:, :n_main].astype(BF16)
            wgt = w_ml_in[j][:, n_main:][:, gate_perm].T.astype(BF16)
            gb = ml_gate_bias[j][gate_perm].reshape(ML_GATES, 1)
            w_out = w_ml_out[j].astype(BF16)
            hg = ml_head_gain[j].reshape(1, D_MODEL)
            q, k, v, og, gr = _ml_proj(xp, gain, w_main, wgt, gb, tm_p)
            zc = jnp.zeros((bp, ml_pairs, LANES, ML_V_DIM), F32)
            zn = jnp.zeros((bp, ml_pairs, 1, LANES), F32)
            zm = jnp.zeros((bp, ml_pairs, 1, 2), F32)
            gr = gr.reshape(ml_pairs, 4, bp, tp).transpose(0, 2, 1, 3)
            y, c1, n1, m1 = _mlstm(q, k, v, og, gr, hg, zc, zn, zm, bp, tp, ML_CHUNK)
            xp = _proj_res(y, w_out, xp, tm_p)
            c_p.append(c1.reshape(bp, ML_HEADS, ML_QK_DIM, ML_V_DIM))
            n_p.append(n1.reshape(bp, ML_HEADS, ML_QK_DIM))
            m_p.append(m1.reshape(bp, ML_HEADS))
            q, k, v, og, gr = _ml_proj(xs, gain, w_main, wgt, gb, tm_s)
            gr = gr.reshape(ML_GATES, bs, ts)
            pad_i = jnp.full((ML_GATES, bs, SAMPLE_PAD - ts), -jnp.inf, F32)
            pad_f = jnp.zeros((ML_GATES, bs, SAMPLE_PAD - ts), F32)
            is_input_gate = (jnp.arange(ML_GATES) % 4 < 2)[:, None, None]
            gr = jnp.concatenate([gr, jnp.where(is_input_gate, pad_i, pad_f)], axis=-1)
            gr = gr.reshape(ml_pairs, 4, bs, SAMPLE_PAD).transpose(0, 2, 1, 3)
            y, c2, n2, m2 = _mlstm(
                _pad_tokens(q, bs, ts), _pad_tokens(k, bs, ts), _pad_tokens(v, bs, ts),
                _pad_tokens(og, bs, ts), gr, hg,
                state_C[j].reshape(bs, ml_pairs, LANES, ML_V_DIM),
                state_n[j].reshape(bs, ml_pairs, 1, LANES),
                state_m[j].reshape(bs, ml_pairs, 1, 2), bs, SAMPLE_PAD, SAMPLE_PAD)
            y = y.reshape(bs, SAMPLE_PAD, D_MODEL)[:, :ts].reshape(bs * ts, D_MODEL)
            xs = _proj_res(y, w_out, xs, tm_s)
            c_s.append(c2.reshape(bs, ML_HEADS, ML_QK_DIM, ML_V_DIM))
            n_s.append(n2.reshape(bs, ML_HEADS, ML_QK_DIM))
            m_s.append(m2.reshape(bs, ML_HEADS))
        fg = norm_ffn[i].reshape(1, D_MODEL)
        wgu = ffn_w_gate_up[i].astype(BF16)
        wd = ffn_w_down[i].astype(BF16)
        xp = _ffn(xp, fg, wgu, wd, tm_p)
        xs = _ffn(xs, fg, wgu, wd, tm_s)

    return (xp.reshape(bp, tp, D_MODEL), xs.reshape(bs, ts, D_MODEL),
            jnp.stack(kp_rows), jnp.stack(vp_rows),
            jnp.stack(c_p), jnp.stack(n_p), jnp.stack(m_p),
            jnp.stack(ks_rows), jnp.stack(vs_rows),
            jnp.stack(c_s), jnp.stack(n_s), jnp.stack(m_s))
```

```python
import functools

import jax
import jax.numpy as jnp
from jax import lax
from jax.experimental import pallas as pl
from jax.experimental.pallas import tpu as pltpu

D_MODEL = 1024
SB_HEADS = 16
SB_HEAD_DIM = 64
ML_HEADS = 8
ML_V_DIM = 128
ML_QK_DIM = 64
ML_CHUNK = 128
ML_GATE_CAP = 15.0
RMS_EPS = 1e-6
PAGE_SIZE = 128

LANES = 128
SUBLANES = 8
NORM_GROUP = 256
VMEM_LIMIT = 56 * 1024 * 1024

BF16 = jnp.bfloat16
F32 = jnp.float32


def _params(*sem):
    return pltpu.CompilerParams(dimension_semantics=sem, vmem_limit_bytes=VMEM_LIMIT)


def _resident(shape):
    nd = len(shape)
    return pl.BlockSpec(shape, lambda *_: (0,) * nd, pipeline_mode=pl.Buffered(1))


def _dot(a, b):
    return jnp.dot(a, b, preferred_element_type=F32)


def _dot_nt(a, b):
    return lax.dot_general(a, b, (((1,), (1,)), ((), ())), preferred_element_type=F32)


def _dot_tn(a, b):
    return lax.dot_general(a, b, (((0,), (0,)), ((), ())), preferred_element_type=F32)


def _rms_rows(x, gain_row):
    ms = jnp.mean(x * x, axis=-1, keepdims=True)
    return x * lax.rsqrt(ms + RMS_EPS) * gain_row


def _log_sigmoid(z):
    return jnp.minimum(z, 0.0) - jnp.log(1.0 + jnp.exp(-jnp.abs(z)))


def _minus_from_key_on(n):
    r = lax.broadcasted_iota(jnp.int32, (n, n), 0)
    c = lax.broadcasted_iota(jnp.int32, (n, n), 1)
    return jnp.where(r >= c, -1.0, 0.0).astype(BF16)


def _stick_break(zs, masks, carries, minus_from):
    parts = []
    for group in zs:
        row = []
        for z, mask in zip(group, masks):
            sp = jnp.maximum(z, 0.0) + jnp.log(1.0 + jnp.exp(-jnp.abs(z)))
            if mask is not None:
                sp = jnp.where(mask, sp, 0.0)
            row.append((z, _dot(sp.astype(BF16), minus_from)))
        parts.append(row)
    out = []
    for row, carry in zip(parts, carries):
        weights = []
        for (z, tail), mask in zip(row, masks):
            a = jnp.exp(z + (tail + carry))
            if mask is not None:
                a = jnp.where(mask, a, 0.0)
            weights.append(a)
            carry = carry + tail[:, :1]
        out.append((weights, carry))
    return out


def _sb_qkv_body(x_ref, g_ref, wq_ref, wkt_ref, wvt_ref, qg_ref, kg_ref, q_ref, kt_ref, vt_ref,
                 h_ref):
    h_ref[...] = _rms_rows(x_ref[...], g_ref[...]).astype(BF16)
    r = lax.broadcasted_iota(jnp.int32, (NORM_GROUP, NORM_GROUP), 0) // SB_HEAD_DIM
    c = lax.broadcasted_iota(jnp.int32, (NORM_GROUP, NORM_GROUP), 1) // SB_HEAD_DIM
    same_head = jnp.where(r == c, 1.0, 0.0).astype(BF16)
    scale = SB_HEAD_DIM ** -0.5
    for ci in range(D_MODEL // NORM_GROUP):
        lo, hi = ci * NORM_GROUP, (ci + 1) * NORM_GROUP
        yq = _dot(h_ref[...], wq_ref[:, lo:hi])
        ss = _dot((yq * yq).astype(BF16), same_head)
        qn = yq * lax.rsqrt(ss * (1.0 / SB_HEAD_DIM) + RMS_EPS) * qg_ref[:, lo:hi]
        q_ref[:, lo:hi] = (qn * scale).astype(BF16)
    tm = h_ref.shape[0]
    yk = _dot_nt(wkt_ref[...], h_ref[...]).reshape(SB_HEADS, SB_HEAD_DIM, tm)
    ss = jnp.mean(yk * yk, axis=1, keepdims=True)
    kt_ref[...] = (yk * lax.rsqrt(ss + RMS_EPS) * kg_ref[...]).reshape(D_MODEL, tm)
    vt_ref[...] = _dot_nt(wvt_ref[...], h_ref[...])


def _sb_qkv(x, gain, wq, wkt, wvt, qg, kg, tm, seq):
    m = x.shape[0]
    per_seq = seq // tm
    row = lambda i: (i, 0)
    feat = lambda i: (i // per_seq, 0, i % per_seq)
    return pl.pallas_call(
        _sb_qkv_body,
        grid=(m // tm,),
        in_specs=[pl.BlockSpec((tm, D_MODEL), row),
                  _resident((1, D_MODEL)),
                  _resident((D_MODEL, D_MODEL)),
                  _resident((D_MODEL, D_MODEL)),
                  _resident((D_MODEL, D_MODEL)),
                  _resident((1, D_MODEL)),
                  _resident((1, SB_HEAD_DIM, 1))],
        out_specs=[pl.BlockSpec((tm, D_MODEL), row),
                   pl.BlockSpec((None, D_MODEL, tm), feat),
                   pl.BlockSpec((None, D_MODEL, tm), feat)],
        out_shape=[jax.ShapeDtypeStruct((m, D_MODEL), BF16),
                   jax.ShapeDtypeStruct((m // seq, D_MODEL, seq), F32),
                   jax.ShapeDtypeStruct((m // seq, D_MODEL, seq), F32)],
        scratch_shapes=[pltpu.VMEM((tm, D_MODEL), BF16)],
        compiler_params=_params("arbitrary"),
        name="sb_qkv",
    )(x, gain, wq, wkt, wvt, qg, kg)


def _sb_attn_body(bias_ref, q_ref, kt_ref, vt_ref, o_ref,
                  kb_ref, vb_ref, up_ref, qm_ref, acc_ref, carry_ref, *, seq, qblk, kblk):
    hp = pl.program_id(1)
    lane = lax.broadcasted_iota(jnp.int32, (1, LANES), 1)
    first = lane < SB_HEAD_DIM
    bias = (bias_ref[2 * hp], bias_ref[2 * hp + 1])
    kb_ref[...] = kt_ref[...].astype(BF16)
    for t in range(seq // LANES):
        vb_ref[t * LANES:(t + 1) * LANES, :] = (
            vt_ref[:, t * LANES:(t + 1) * LANES].T.astype(BF16))
    up_ref[...] = _minus_from_key_on(kblk)
    r = lax.broadcasted_iota(jnp.int32, (qblk, kblk), 0)
    c = lax.broadcasted_iota(jnp.int32, (qblk, kblk), 1)
    ratio = qblk // kblk
    own_keys = [c + d * kblk < r for d in reversed(range(ratio))]

    def tile(p, masks):
        off = pl.multiple_of(p * qblk, qblk)
        kb = kb_ref[:, pl.ds(off, qblk)]
        vb = vb_ref[pl.ds(off, qblk), :]
        zs = []
        for hd in range(2):
            z = _dot(qm_ref[hd], kb) + bias[hd]
            zs.append([z[:, d * kblk:(d + 1) * kblk] for d in reversed(range(ratio))])
        (w0, c0), (w1, c1) = _stick_break(zs, masks, (carry_ref[0], carry_ref[1]), up_ref[...])
        p0 = _dot(jnp.concatenate(w0[::-1], axis=1).astype(BF16), vb)
        p1 = _dot(jnp.concatenate(w1[::-1], axis=1).astype(BF16), vb)
        acc_ref[...] += jnp.where(first, p0, p1)
        carry_ref[0] = c0
        carry_ref[1] = c1

    def query_block(qi, _):
        qoff = pl.multiple_of(qi * qblk, qblk)
        q = q_ref[pl.ds(qoff, qblk), :]
        qm_ref[0] = jnp.where(first, q, jnp.zeros_like(q))
        qm_ref[1] = jnp.where(first, jnp.zeros_like(q), q)
        acc_ref[...] = jnp.zeros_like(acc_ref)
        carry_ref[...] = jnp.zeros_like(carry_ref)
        tile(qi, own_keys)

        def older_tile(t, _):
            tile(qi - 1 - t, [None] * ratio)
            return 0

        lax.fori_loop(0, qi, older_tile, 0)
        o_ref[pl.ds(qoff, qblk), :] = acc_ref[...].astype(o_ref.dtype)
        return 0

    lax.fori_loop(0, seq // qblk, query_block, 0)


def _sb_attn(q, kt, vt, bias, batch, seq, qblk, kblk):
    m = q.shape[0]
    pairs = SB_HEADS // 2
    return pl.pallas_call(
        functools.partial(_sb_attn_body, seq=seq, qblk=qblk, kblk=kblk),
        grid=(batch, pairs),
        in_specs=[pl.BlockSpec(memory_space=pltpu.SMEM),
                  pl.BlockSpec((seq, LANES), lambda b, p: (b, p)),
                  pl.BlockSpec((None, LANES, seq), lambda b, p: (b, p, 0)),
                  pl.BlockSpec((None, LANES, seq), lambda b, p: (b, p, 0))],
        out_specs=pl.BlockSpec((seq, LANES), lambda b, p: (b, p)),
        out_shape=jax.ShapeDtypeStruct((m, D_MODEL), BF16),
        scratch_shapes=[pltpu.VMEM((LANES, seq), BF16),
                        pltpu.VMEM((seq, LANES), BF16),
                        pltpu.VMEM((kblk, kblk), BF16),
                        pltpu.VMEM((2, qblk, LANES), BF16),
                        pltpu.VMEM((qblk, LANES), F32),
                        pltpu.VMEM((2, qblk, 1), F32)],
        compiler_params=_params("arbitrary", "arbitrary"),
        name="sb_attn_prompt",
    )(bias, q, kt, vt)


def _sb_sample_body(pt_ref, bias_ref, q_ref, kn_ref, vn_ref, *refs, pages_per_step):
    k_refs = refs[:pages_per_step]
    v_refs = refs[pages_per_step:2 * pages_per_step]
    o_ref = refs[2 * pages_per_step]
    qbd_ref, acc_ref, carry_ref, kpad_ref, vpad_ref = refs[2 * pages_per_step + 1:]
    s = pl.program_id(1)
    nq = kn_ref.shape[-1]
    rows = SB_HEADS * SUBLANES
    upper = _minus_from_key_on(PAGE_SIZE)

    def visit(pages, mask):
        zs = [_dot(qbd_ref[...], k3.astype(BF16).reshape(D_MODEL, PAGE_SIZE)) + bias_ref[...]
              for k3, _ in pages]
        (weights, carry), = _stick_break([zs], [mask] * len(pages), [carry_ref[...]], upper)
        acc = acc_ref[...]
        for a, (_, v3) in zip(weights, pages):
            a3 = a.reshape(SB_HEADS, SUBLANES, PAGE_SIZE)
            acc = acc + lax.dot_general(a3, v3, (((2,), (2,)), ((0,), (0,))),
                                        preferred_element_type=F32)
        acc_ref[...] = acc
        carry_ref[...] = carry

    @pl.when(s == 0)
    def _():
        qrep = jnp.concatenate([q_ref[...]] * (D_MODEL // LANES), axis=1)
        row_head = lax.broadcasted_iota(jnp.int32, (rows, D_MODEL), 0) // SUBLANES
        col_head = lax.broadcasted_iota(jnp.int32, (rows, D_MODEL), 1) // SB_HEAD_DIM
        qbd_ref[...] = jnp.where(row_head == col_head, qrep, 0.0).astype(BF16)
        acc_ref[...] = jnp.zeros_like(acc_ref)
        carry_ref[...] = jnp.zeros_like(carry_ref)
        kpad_ref[...] = jnp.zeros_like(kpad_ref)
        vpad_ref[...] = jnp.zeros_like(vpad_ref)
        kpad_ref[:, :, 0:nq] = kn_ref[...]
        vpad_ref[:, :, 0:nq] = vn_ref[...]
        key = lax.broadcasted_iota(jnp.int32, (rows, PAGE_SIZE), 1)
        qry = lax.broadcasted_iota(jnp.int32, (rows, PAGE_SIZE), 0) % SUBLANES
        visit([(kpad_ref[...], vpad_ref[...])], key < qry)

    @pl.when(s > 0)
    def _():
        visit([(k_ref[...], v_ref[...]) for k_ref, v_ref in zip(k_refs, v_refs)], None)

    @pl.when(s == pl.num_programs(1) - 1)
    def _():
        o_ref[...] = acc_ref[...]


def _sb_sample(q, kn, vn, cache_k, cache_v, page_table, bias_col, pages_per_step):
    batch = q.shape[0]
    nq = kn.shape[-1]
    n_pages = page_table.shape[1]
    steps = n_pages // pages_per_step
    rows = SB_HEADS * SUBLANES

    def page_spec(i):
        def index(b, s, pt):
            first = jnp.maximum(s - 1, 0) * pages_per_step
            return (pt[b, n_pages - 1 - (first + i)], 0, 0, 0)
        return pl.BlockSpec((None, SB_HEADS, SB_HEAD_DIM, PAGE_SIZE), index)

    per_seq = lambda b, s, pt: (b, 0, 0, 0)
    new_spec = pl.BlockSpec((None, SB_HEADS, SB_HEAD_DIM, nq), per_seq)
    out_spec = pl.BlockSpec((None, SB_HEADS, SUBLANES, SB_HEAD_DIM), per_seq)
    page_specs = [page_spec(i) for i in range(pages_per_step)]
    grid_spec = pltpu.PrefetchScalarGridSpec(
        num_scalar_prefetch=1,
        grid=(batch, steps + 1),
        in_specs=[pl.BlockSpec((rows, 1), lambda b, s, pt: (0, 0)),
                  pl.BlockSpec((None, rows, LANES), lambda b, s, pt: (b, 0, 0)),
                  new_spec, new_spec]
        + page_specs + page_specs,
        out_specs=out_spec,
        scratch_shapes=[pltpu.VMEM((rows, D_MODEL), BF16),
                        pltpu.VMEM((SB_HEADS, SUBLANES, SB_HEAD_DIM), F32),
                        pltpu.VMEM((rows, 1), F32),
                        pltpu.VMEM((SB_HEADS, SB_HEAD_DIM, PAGE_SIZE), F32),
                        pltpu.VMEM((SB_HEADS, SB_HEAD_DIM, PAGE_SIZE), F32)],
    )
    return pl.pallas_call(
        functools.partial(_sb_sample_body, pages_per_step=pages_per_step),
        grid_spec=grid_spec,
        out_shape=jax.ShapeDtypeStruct((batch, SB_HEADS, SUBLANES, SB_HEAD_DIM), F32),
        compiler_params=_params("arbitrary", "arbitrary"),
        name="sb_attn_sample",
    )(page_table, bias_col, q, kn, vn,
      *([cache_k] * pages_per_step), *([cache_v] * pages_per_step))


def _proj_res_body(a_ref, w_ref, x_ref, o_ref):
    o_ref[...] = x_ref[...] + _dot(a_ref[...].astype(BF16), w_ref[...])


def _proj_res(a, w, x, tm):
    m = x.shape[0]
    row = lambda i: (i, 0)
    return pl.pallas_call(
        _proj_res_body,
        grid=(m // tm,),
        in_specs=[pl.BlockSpec((tm, D_MODEL), row),
                  _resident((D_MODEL, D_MODEL)),
                  pl.BlockSpec((tm, D_MODEL), row)],
        out_specs=pl.BlockSpec((tm, D_MODEL), row),
        out_shape=jax.ShapeDtypeStruct((m, D_MODEL), F32),
        compiler_params=_params("arbitrary"),
        name="proj_res",
    )(a, w, x)


def _ffn_body(x_ref, g_ref, wgu_ref, wd_ref, o_ref, h_ref, *, hidden, chunk):
    x = x_ref[...]
    h_ref[...] = _rms_rows(x, g_ref[...]).astype(BF16)
    o_ref[...] = x

    def step(ci, _):
        off = pl.multiple_of(ci * chunk, chunk)
        off_up = pl.multiple_of(hidden + ci * chunk, chunk)
        gate = _dot(h_ref[...], wgu_ref[:, pl.ds(off, chunk)])
        up = _dot(h_ref[...], wgu_ref[:, pl.ds(off_up, chunk)])
        act = (gate * jax.nn.sigmoid(gate) * up).astype(BF16)
        o_ref[...] += _dot(act, wd_ref[pl.ds(off, chunk), :])
        return 0

    lax.fori_loop(0, hidden // chunk, step, 0, unroll=True)


def _ffn(x, gain, wgu, wd, tm, chunk=256):
    m = x.shape[0]
    hidden = wd.shape[0]
    row = lambda i: (i, 0)
    return pl.pallas_call(
        functools.partial(_ffn_body, hidden=hidden, chunk=chunk),
        grid=(m // tm,),
        in_specs=[pl.BlockSpec((tm, D_MODEL), row),
                  _resident((1, D_MODEL)),
                  _resident((D_MODEL, 2 * hidden)),
                  _resident((hidden, D_MODEL))],
        out_specs=pl.BlockSpec((tm, D_MODEL), row),
        out_shape=jax.ShapeDtypeStruct((m, D_MODEL), F32),
        scratch_shapes=[pltpu.VMEM((tm, D_MODEL), BF16)],
        compiler_params=_params("arbitrary"),
        name="ffn",
    )(x, gain, wgu, wd)


ML_QK_W = ML_HEADS * ML_QK_DIM
ML_GATES = 2 * ML_HEADS


def _ml_proj_body(x_ref, g_ref, w_ref, wgt_ref, gb_ref, q_ref, k_ref, v_ref, og_ref, gr_ref,
                  h_ref):
    h_ref[...] = _rms_rows(x_ref[...], g_ref[...]).astype(BF16)
    q_ref[...] = _dot(h_ref[...], w_ref[:, 0:ML_QK_W]).astype(BF16)
    k_ref[...] = (_dot(h_ref[...], w_ref[:, ML_QK_W:2 * ML_QK_W])
                  * (ML_QK_DIM ** -0.5)).astype(BF16)
    v_ref[...] = _dot(h_ref[...], w_ref[:, 2 * ML_QK_W:2 * ML_QK_W + D_MODEL]).astype(BF16)
    og_ref[...] = _dot(h_ref[...], w_ref[:, 2 * ML_QK_W + D_MODEL:2 * ML_QK_W + 2 * D_MODEL])
    g = _dot_nt(wgt_ref[...], h_ref[...]) + gb_ref[...]
    g = ML_GATE_CAP * jnp.tanh(g * (1.0 / ML_GATE_CAP))
    is_input_gate = (lax.broadcasted_iota(jnp.int32, g.shape, 0) % 4) < 2
    gr_ref[...] = jnp.where(is_input_gate, g, _log_sigmoid(g))


def _ml_proj(x, gain, w, wgt, gb, tm):
    m = x.shape[0]
    row = lambda i: (i, 0)
    return pl.pallas_call(
        _ml_proj_body,
        grid=(m // tm,),
        in_specs=[pl.BlockSpec((tm, D_MODEL), row),
                  _resident((1, D_MODEL)),
                  _resident((D_MODEL, 2 * ML_QK_W + 2 * D_MODEL)),
                  _resident((ML_GATES, D_MODEL)),
                  _resident((ML_GATES, 1))],
        out_specs=[pl.BlockSpec((tm, ML_QK_W), row),
                   pl.BlockSpec((tm, ML_QK_W), row),
                   pl.BlockSpec((tm, D_MODEL), row),
                   pl.BlockSpec((tm, D_MODEL), row),
                   pl.BlockSpec((ML_GATES, tm), lambda i: (0, i))],
        out_shape=[jax.ShapeDtypeStruct((m, ML_QK_W), BF16),
                   jax.ShapeDtypeStruct((m, ML_QK_W), BF16),
                   jax.ShapeDtypeStruct((m, D_MODEL), BF16),
                   jax.ShapeDtypeStruct((m, D_MODEL), F32),
                   jax.ShapeDtypeStruct((ML_GATES, m), F32)],
        scratch_shapes=[pltpu.VMEM((tm, D_MODEL), BF16)],
        compiler_params=_params("arbitrary"),
        name="ml_proj",
    )(x, gain, w, wgt, gb)


def _mlstm_body(q_ref, k_ref, v_ref, og_ref, gr_ref, hg_ref, c0_ref, n0_ref, m0_ref,
                y_ref, c_ref, n_ref, m_ref, *, seq, chunk):
    lane = lax.broadcasted_iota(jnp.int32, (1, LANES), 1)
    sub = lax.broadcasted_iota(jnp.int32, (LANES, 1), 0)
    head_lanes = (lane < ML_QK_DIM, lane >= ML_QK_DIM)
    r = lax.broadcasted_iota(jnp.int32, (chunk, chunk), 0)
    c = lax.broadcasted_iota(jnp.int32, (chunk, chunk), 1)
    causal = c <= r
    eye = c == r
    cum = jnp.where(r <= c, 1.0, 0.0)

    def to_col(x_row):
        return jnp.sum(jnp.where(eye, x_row, 0.0), axis=-1, keepdims=True)

    two = lax.broadcasted_iota(jnp.int32, (1, 2), 1)
    pairs = ML_HEADS // 2
    heads = [(p, hd) for p in range(pairs) for hd in range(2)]
    c_ref[...] = c0_ref[...]
    n_ref[...] = n0_ref[...]
    m_ref[...] = m0_ref[...]

    def step(ci, _):
        rows = pl.ds(pl.multiple_of(ci * chunk, chunk), chunk)
        q = [q_ref[rows, p * LANES:(p + 1) * LANES] for p in range(pairs)]
        k = [k_ref[rows, p * LANES:(p + 1) * LANES] for p in range(pairs)]
        g = [gr_ref[p, :, rows] for p in range(pairs)]
        cmat = [c_ref[p] for p in range(pairs)]
        nrow = [n_ref[p] for p in range(pairs)]
        m_prev = {(p, hd): m_ref[p][:, hd:hd + 1] for p, hd in heads}
        vcols = {(p, hd): slice((2 * p + hd) * ML_V_DIM, (2 * p + hd + 1) * ML_V_DIM)
                 for p, hd in heads}
        qm = {(p, hd): jnp.where(head_lanes[hd], q[p], jnp.zeros_like(q[p])) for p, hd in heads}
        qk = {h: _dot_nt(qm[h], k[h[0]]) for h in heads}
        qc = {h: _dot(qm[h], cmat[h[0]].astype(BF16)) for h in heads}
        b_rows = [jnp.dot(g[p][2:4, :], cum, precision=lax.Precision.HIGHEST,
                          preferred_element_type=F32) for p in range(pairs)]
        li_row, b_row, m_t, m_inter, s = {}, {}, {}, {}, {}
        for h in heads:
            p, hd = h
            li_row[h] = g[p][hd:hd + 1, :]
            b_row[h] = b_rows[p][hd:hd + 1, :]
            b_col = to_col(b_row[h])
            dmat = jnp.where(causal, b_col - b_row[h] + li_row[h], -jnp.inf)
            m_inter[h] = b_col + m_prev[h]
            m_t[h] = jnp.maximum(m_inter[h], jnp.max(dmat, axis=-1, keepdims=True))
            s[h] = qk[h] * jnp.exp(dmat - m_t[h])
        sv = {h: _dot(s[h].astype(BF16), v_ref[rows, vcols[h]]) for h in heads}
        for h in heads:
            inter = jnp.exp(m_inter[h] - m_t[h])
            num = inter * qc[h] + sv[h]
            qn = jnp.sum(qm[h].astype(F32) * nrow[h[0]], axis=-1, keepdims=True)
            den = inter * qn + jnp.sum(s[h], axis=-1, keepdims=True)
            hh = num / jnp.maximum(jnp.abs(den), jnp.exp(-m_t[h]))
            hn = hh * lax.rsqrt(jnp.mean(hh * hh, axis=-1, keepdims=True) + RMS_EPS)
            hn = hn * hg_ref[:, vcols[h]]
            y_ref[rows, vcols[h]] = (hn * jax.nn.sigmoid(og_ref[rows, vcols[h]])).astype(y_ref.dtype)
        m_new, decay, kw = {}, {}, {}
        for h in heads:
            p, hd = h
            m_new[h] = m_t[h][chunk - 1:chunk, :]
            b_last = b_row[h][:, chunk - 1:chunk]
            w_col = to_col(jnp.exp(b_last - b_row[h] + li_row[h] - m_new[h]))
            kw[h] = jnp.where(head_lanes[hd], k[p].astype(F32) * w_col, 0.0)
            decay[h] = jnp.exp(b_last + m_prev[h] - m_new[h])
        d_c = {h: _dot_tn(kw[h].astype(BF16), v_ref[rows, vcols[h]]) for h in heads}
        for p in range(pairs):
            a, b = (p, 0), (p, 1)
            c_ref[p] = (jnp.where(sub < ML_QK_DIM, decay[a], decay[b]) * cmat[p]
                        + d_c[a] + d_c[b])
            n_ref[p] = (jnp.where(head_lanes[0], decay[a], decay[b]) * nrow[p]
                        + jnp.sum(kw[a], axis=0, keepdims=True)
                        + jnp.sum(kw[b], axis=0, keepdims=True))
            m_ref[p] = jnp.where(two == 0, m_new[a], m_new[b])
        return 0

    lax.fori_loop(0, seq // chunk, step, 0)


def _mlstm(q, k, v, og, gr, head_gain, c0, n0, m0, batch, seq, chunk):
    m = q.shape[0]
    pairs = ML_HEADS // 2
    tok = lambda b: (b, 0)
    st = lambda b: (b, 0, 0, 0)
    states = [pl.BlockSpec((None, pairs, LANES, ML_V_DIM), st),
              pl.BlockSpec((None, pairs, 1, LANES), st),
              pl.BlockSpec((None, pairs, 1, 2), st)]
    return pl.pallas_call(
        functools.partial(_mlstm_body, seq=seq, chunk=chunk),
        grid=(batch,),
        in_specs=[pl.BlockSpec((seq, ML_QK_W), tok),
                  pl.BlockSpec((seq, ML_QK_W), tok),
                  pl.BlockSpec((seq, D_MODEL), tok),
                  pl.BlockSpec((seq, D_MODEL), tok),
                  pl.BlockSpec((pairs, None, 4, seq), lambda b: (0, b, 0, 0)),
                  _resident((1, D_MODEL))] + states,
        out_specs=[pl.BlockSpec((seq, D_MODEL), tok)] + states,
        out_shape=[jax.ShapeDtypeStruct((m, D_MODEL), BF16),
                   jax.ShapeDtypeStruct((batch, pairs, LANES, ML_V_DIM), F32),
                   jax.ShapeDtypeStruct((batch, pairs, 1, LANES), F32),
                   jax.ShapeDtypeStruct((batch, pairs, 1, 2), F32)],
        compiler_params=_params("arbitrary"),
        name="mlstm",
    )(q, k, v, og, gr, head_gain, c0, n0, m0)


SAMPLE_PAD = 16
PAGES_PER_STEP = 8
SB_QUERY_BLOCK = 512
SB_KEY_BLOCK = 256


def _pad_tokens(a, batch, seq, value=0.0):
    a = a.reshape(batch, seq, a.shape[-1])
    a = jnp.pad(a, ((0, 0), (0, SAMPLE_PAD - seq), (0, 0)), constant_values=value)
    return a.reshape(batch * SAMPLE_PAD, a.shape[-1])


def _heads_last(xt, batch, seq):
    return xt.reshape(batch, SB_HEADS, SB_HEAD_DIM, seq).transpose(0, 3, 1, 2)


def kernel(x_prompt, x_sample, cache_k, cache_v, state_C, state_n, state_m, page_table,
           norm_mix, norm_ffn, w_sb_in, sb_q_gain, sb_k_gain, sb_logit_bias, w_sb_out,
           w_ml_in, ml_gate_bias, ml_head_gain, w_ml_out, ffn_w_gate_up, ffn_w_down):
    bp, tp, _ = x_prompt.shape
    bs, ts, _ = x_sample.shape
    depth = norm_mix.shape[0]
    n_phys = cache_k.shape[1]
    xp = x_prompt.reshape(bp * tp, D_MODEL)
    xs = x_sample.reshape(bs * ts, D_MODEL)
    tm_p, tm_s = 512, bs * ts
    ml_pairs = ML_HEADS // 2
    gate_perm = jnp.array([g for p in range(ml_pairs)
                           for g in (2 * p, 2 * p + 1, ML_HEADS + 2 * p, ML_HEADS + 2 * p + 1)])
    ck = cache_k.transpose(0, 1, 3, 4, 2).reshape(-1, SB_HEADS, SB_HEAD_DIM, PAGE_SIZE)
    cv = cache_v.transpose(0, 1, 3, 4, 2).reshape(-1, SB_HEADS, SB_HEAD_DIM, PAGE_SIZE)

    kp_rows, vp_rows, ks_rows, vs_rows = [], [], [], []
    c_p, n_p, m_p, c_s, n_s, m_s = [], [], [], [], [], []
    for i in range(depth):
        j = i // 2
        gain = norm_mix[i].reshape(1, D_MODEL)
        if i % 2 == 0:
            wq = w_sb_in[j][:, :D_MODEL].astype(BF16)
            wkt = w_sb_in[j][:, D_MODEL:2 * D_MODEL].T.astype(BF16)
            wvt = w_sb_in[j][:, 2 * D_MODEL:].T.astype(BF16)
            w_out = w_sb_out[j].astype(BF16)
            qg = jnp.tile(sb_q_gain[j], SB_HEADS).reshape(1, D_MODEL)
            kg = sb_k_gain[j].reshape(1, SB_HEAD_DIM, 1)
            bias = sb_logit_bias[j]
            q, kt, vt = _sb_qkv(xp, gain, wq, wkt, wvt, qg, kg, tm_p, tp)
            att = _sb_attn(q, kt, vt, bias, bp, tp, SB_QUERY_BLOCK, SB_KEY_BLOCK)
            xp = _proj_res(att, w_out, xp, tm_p)
            kp_rows.append(_heads_last(kt, bp, tp))
            vp_rows.append(_heads_last(vt, bp, tp))
            q, kt, vt = _sb_qkv(xs, gain, wq, wkt, wvt, qg, kg, tm_s, tm_s)
            q8 = q.astype(F32).reshape(bs, ts, SB_HEADS, SB_HEAD_DIM).transpose(0, 2, 1, 3)
            q8 = jnp.pad(q8, ((0, 0), (0, 0), (0, SUBLANES - ts), (0, 0)))
            q8 = jnp.concatenate([q8, q8], axis=-1).reshape(bs, SB_HEADS * SUBLANES, LANES)
            new = lambda xt: xt.reshape(SB_HEADS, SB_HEAD_DIM, bs, ts).transpose(2, 0, 1, 3)
            bias_col = jnp.repeat(bias, SUBLANES).reshape(SB_HEADS * SUBLANES, 1)
            att = _sb_sample(q8, new(kt), new(vt), ck, cv, page_table + j * n_phys,
                             bias_col, PAGES_PER_STEP)
            att = att[:, :, :ts].transpose(0, 2, 1, 3).reshape(bs * ts, D_MODEL)
            xs = _proj_res(att, w_out, xs, tm_s)
            ks_rows.append(_heads_last(kt, 1, bs * ts).reshape(bs, ts, SB_HEADS, SB_HEAD_DIM))
            vs_rows.append(_heads_last(vt, 1, bs * ts).reshape(bs, ts, SB_HEADS, SB_HEAD_DIM))
        else:
            n_main = 2 * ML_QK_W + 2 * D_MODEL
            w_main = w_ml_in[j][:, :n_main].astype(BF16)
            wgt = w_ml_in[j][:, n_main:][:, gate_perm].T.astype(BF16)
            gb = ml_gate_bias[j][gate_perm].reshape(ML_GATES, 1)
            w_out = w_ml_out[j].astype(BF16)
            hg = ml_head_gain[j].reshape(1, D_MODEL)
            q, k, v, og, gr = _ml_proj(xp, gain, w_main, wgt, gb, tm_p)
            zc = jnp.zeros((bp, ml_pairs, LANES, ML_V_DIM), F32)
            zn = jnp.zeros((bp, ml_pairs, 1, LANES), F32)
            zm = jnp.zeros((bp, ml_pairs, 1, 2), F32)
            gr = gr.reshape(ml_pairs, 4, bp, tp).transpose(0, 2, 1, 3)
            y, c1, n1, m1 = _mlstm(q, k, v, og, gr, hg, zc, zn, zm, bp, tp, ML_CHUNK)
            xp = _proj_res(y, w_out, xp, tm_p)
            c_p.append(c1.reshape(bp, ML_HEADS, ML_QK_DIM, ML_V_DIM))
            n_p.append(n1.reshape(bp, ML_HEADS, ML_QK_DIM))
            m_p.append(m1.reshape(bp, ML_HEADS))
            q, k, v, og, gr = _ml_proj(xs, gain, w_main, wgt, gb, tm_s)
            gr = gr.reshape(ML_GATES, bs, ts)
            pad_i = jnp.full((ML_GATES, bs, SAMPLE_PAD - ts), -jnp.inf, F32)
            pad_f = jnp.zeros((ML_GATES, bs, SAMPLE_PAD - ts), F32)
            is_input_gate = (jnp.arange(ML_GATES) % 4 < 2)[:, None, None]
            gr = jnp.concatenate([gr, jnp.where(is_input_gate, pad_i, pad_f)], axis=-1)
            gr = gr.reshape(ml_pairs, 4, bs, SAMPLE_PAD).transpose(0, 2, 1, 3)
            y, c2, n2, m2 = _mlstm(
                _pad_tokens(q, bs, ts), _pad_tokens(k, bs, ts), _pad_tokens(v, bs, ts),
                _pad_tokens(og, bs, ts), gr, hg,
                state_C[j].reshape(bs, ml_pairs, LANES, ML_V_DIM),
                state_n[j].reshape(bs, ml_pairs, 1, LANES),
                state_m[j].reshape(bs, ml_pairs, 1, 2), bs, SAMPLE_PAD, SAMPLE_PAD)
            y = y.reshape(bs, SAMPLE_PAD, D_MODEL)[:, :ts].reshape(bs * ts, D_MODEL)
            xs = _proj_res(y, w_out, xs, tm_s)
            c_s.append(c2.reshape(bs, ML_HEADS, ML_QK_DIM, ML_V_DIM))
            n_s.append(n2.reshape(bs, ML_HEADS, ML_QK_DIM))
            m_s.append(m2.reshape(bs, ML_HEADS))
        fg = norm_ffn[i].reshape(1, D_MODEL)
        wgu = ffn_w_gate_up[i].astype(BF16)
        wd = ffn_w_down[i].astype(BF16)
        xp = _ffn(xp, fg, wgu, wd, tm_p)
        xs = _ffn(xs, fg, wgu, wd, tm_s)

    return (xp.reshape(bp, tp, D_MODEL), xs.reshape(bs, ts, D_MODEL),
            jnp.stack(kp_rows), jnp.stack(vp_rows),
            jnp.stack(c_p), jnp.stack(n_p), jnp.stack(m_p),
            jnp.stack(ks_rows), jnp.stack(vs_rows),
            jnp.stack(c_s), jnp.stack(n_s), jnp.stack(m_s))
```

```python
import functools

import jax
import jax.numpy as jnp
from jax import lax
from jax.experimental import pallas as pl
from jax.experimental.pallas import tpu as pltpu

D_MODEL = 1024
SB_HEADS = 16
SB_HEAD_DIM = 64
ML_HEADS = 8
ML_V_DIM = 128
ML_QK_DIM = 64
ML_CHUNK = 128
ML_GATE_CAP = 15.0
RMS_EPS = 1e-6
PAGE_SIZE = 128

LANES = 128
SUBLANES = 8
NORM_GROUP = 256
VMEM_LIMIT = 56 * 1024 * 1024

BF16 = jnp.bfloat16
F32 = jnp.float32


def _params(*sem):
    return pltpu.CompilerParams(dimension_semantics=sem, vmem_limit_bytes=VMEM_LIMIT)


def _resident(shape):
    nd = len(shape)
    return pl.BlockSpec(shape, lambda *_: (0,) * nd, pipeline_mode=pl.Buffered(1))


def _dot(a, b):
    return jnp.dot(a, b, preferred_element_type=F32)


def _dot_nt(a, b):
    return lax.dot_general(a, b, (((1,), (1,)), ((), ())), preferred_element_type=F32)


def _dot_tn(a, b):
    return lax.dot_general(a, b, (((0,), (0,)), ((), ())), preferred_element_type=F32)


def _rms_rows(x, gain_row):
    ms = jnp.mean(x * x, axis=-1, keepdims=True)
    return x * lax.rsqrt(ms + RMS_EPS) * gain_row


def _log_sigmoid(z):
    return jnp.minimum(z, 0.0) - jnp.log(1.0 + jnp.exp(-jnp.abs(z)))


def _minus_from_key_on(n):
    r = lax.broadcasted_iota(jnp.int32, (n, n), 0)
    c = lax.broadcasted_iota(jnp.int32, (n, n), 1)
    return jnp.where(r >= c, -1.0, 0.0).astype(BF16)


def _stick_break(zs, masks, carries, minus_from):
    parts = []
    for group in zs:
        row = []
        for z, mask in zip(group, masks):
            sp = jnp.maximum(z, 0.0) + jnp.log(1.0 + jnp.exp(-jnp.abs(z)))
            if mask is not None:
                sp = jnp.where(mask, sp, 0.0)
            row.append((z, _dot(sp.astype(BF16), minus_from)))
        parts.append(row)
    out = []
    for row, carry in zip(parts, carries):
        weights = []
        for (z, tail), mask in zip(row, masks):
            a = jnp.exp(z + (tail + carry))
            if mask is not None:
                a = jnp.where(mask, a, 0.0)
            weights.append(a)
            carry = carry + tail[:, :1]
        out.append((weights, carry))
    return out


def _sb_qkv_body(x_ref, g_ref, wq_ref, wkt_ref, wvt_ref, qg_ref, kg_ref, q_ref, kt_ref, vt_ref,
                 h_ref):
    h_ref[...] = _rms_rows(x_ref[...], g_ref[...]).astype(BF16)
    r = lax.broadcasted_iota(jnp.int32, (NORM_GROUP, NORM_GROUP), 0) // SB_HEAD_DIM
    c = lax.broadcasted_iota(jnp.int32, (NORM_GROUP, NORM_GROUP), 1) // SB_HEAD_DIM
    same_head = jnp.where(r == c, 1.0, 0.0).astype(BF16)
    scale = SB_HEAD_DIM ** -0.5
    for ci in range(D_MODEL // NORM_GROUP):
        lo, hi = ci * NORM_GROUP, (ci + 1) * NORM_GROUP
        yq = _dot(h_ref[...], wq_ref[:, lo:hi])
        ss = _dot((yq * yq).astype(BF16), same_head)
        qn = yq * lax.rsqrt(ss * (1.0 / SB_HEAD_DIM) + RMS_EPS) * qg_ref[:, lo:hi]
        q_ref[:, lo:hi] = (qn * scale).astype(BF16)
    tm = h_ref.shape[0]
    yk = _dot_nt(wkt_ref[...], h_ref[...]).reshape(SB_HEADS, SB_HEAD_DIM, tm)
    ss = jnp.mean(yk * yk, axis=1, keepdims=True)
    kt_ref[...] = (yk * lax.rsqrt(ss + RMS_EPS) * kg_ref[...]).reshape(D_MODEL, tm)
    vt_ref[...] = _dot_nt(wvt_ref[...], h_ref[...])


def _sb_qkv(x, gain, wq, wkt, wvt, qg, kg, tm, seq):
    m = x.shape[0]
    per_seq = seq // tm
    row = lambda i: (i, 0)
    feat = lambda i: (i // per_seq, 0, i % per_seq)
    return pl.pallas_call(
        _sb_qkv_body,
        grid=(m // tm,),
        in_specs=[pl.BlockSpec((tm, D_MODEL), row),
                  _resident((1, D_MODEL)),
                  _resident((D_MODEL, D_MODEL)),
                  _resident((D_MODEL, D_MODEL)),
                  _resident((D_MODEL, D_MODEL)),
                  _resident((1, D_MODEL)),
                  _resident((1, SB_HEAD_DIM, 1))],
        out_specs=[pl.BlockSpec((tm, D_MODEL), row),
                   pl.BlockSpec((None, D_MODEL, tm), feat),
                   pl.BlockSpec((None, D_MODEL, tm), feat)],
        out_shape=[jax.ShapeDtypeStruct((m, D_MODEL), BF16),
                   jax.ShapeDtypeStruct((m // seq, D_MODEL, seq), F32),
                   jax.ShapeDtypeStruct((m // seq, D_MODEL, seq), F32)],
        scratch_shapes=[pltpu.VMEM((tm, D_MODEL), BF16)],
        compiler_params=_params("arbitrary"),
        name="sb_qkv",
    )(x, gain, wq, wkt, wvt, qg, kg)


def _sb_attn_body(bias_ref, q_ref, kt_ref, vt_ref, o_ref,
                  kb_ref, vb_ref, up_ref, qm_ref, acc_ref, carry_ref, *, seq, qblk, kblk):
    hp = pl.program_id(1)
    lane = lax.broadcasted_iota(jnp.int32, (1, LANES), 1)
    first = lane < SB_HEAD_DIM
    bias = (bias_ref[2 * hp], bias_ref[2 * hp + 1])
    kb_ref[...] = kt_ref[...].astype(BF16)
    for t in range(seq // LANES):
        vb_ref[t * LANES:(t + 1) * LANES, :] = (
            vt_ref[:, t * LANES:(t + 1) * LANES].T.astype(BF16))
    up_ref[...] = _minus_from_key_on(kblk)
    r = lax.broadcasted_iota(jnp.int32, (qblk, kblk), 0)
    c = lax.broadcasted_iota(jnp.int32, (qblk, kblk), 1)
    ratio = qblk // kblk
    own_keys = [c + d * kblk < r for d in reversed(range(ratio))]

    def tile(p, masks):
        off = pl.multiple_of(p * qblk, qblk)
        kb = kb_ref[:, pl.ds(off, qblk)]
        vb = vb_ref[pl.ds(off, qblk), :]
        zs = []
        for hd in range(2):
            z = _dot(qm_ref[hd], kb) + bias[hd]
            zs.append([z[:, d * kblk:(d + 1) * kblk] for d in reversed(range(ratio))])
        (w0, c0), (w1, c1) = _stick_break(zs, masks, (carry_ref[0], carry_ref[1]), up_ref[...])
        p0 = _dot(jnp.concatenate(w0[::-1], axis=1).astype(BF16), vb)
        p1 = _dot(jnp.concatenate(w1[::-1], axis=1).astype(BF16), vb)
        acc_ref[...] += jnp.where(first, p0, p1)
        carry_ref[0] = c0
        carry_ref[1] = c1

    def query_block(qi, _):
        qoff = pl.multiple_of(qi * qblk, qblk)
        q = q_ref[pl.ds(qoff, qblk), :]
        qm_ref[0] = jnp.where(first, q, jnp.zeros_like(q))
        qm_ref[1] = jnp.where(first, jnp.zeros_like(q), q)
        acc_ref[...] = jnp.zeros_like(acc_ref)
        carry_ref[...] = jnp.zeros_like(carry_ref)
        tile(qi, own_keys)

        def older_tile(t, _):
            tile(qi - 1 - t, [None] * ratio)
            return 0

        lax.fori_loop(0, qi, older_tile, 0)
        o_ref[pl.ds(qoff, qblk), :] = acc_ref[...].astype(o_ref.dtype)
        return 0

    lax.fori_loop(0, seq // qblk, query_block, 0)


def _sb_attn(q, kt, vt, bias, batch, seq, qblk, kblk):
    m = q.shape[0]
    pairs = SB_HEADS // 2
    return pl.pallas_call(
        functools.partial(_sb_attn_body, seq=seq, qblk=qblk, kblk=kblk),
        grid=(batch, pairs),
        in_specs=[pl.BlockSpec(memory_space=pltpu.SMEM),
                  pl.BlockSpec((seq, LANES), lambda b, p: (b, p)),
                  pl.BlockSpec((None, LANES, seq), lambda b, p: (b, p, 0)),
                  pl.BlockSpec((None, LANES, seq), lambda b, p: (b, p, 0))],
        out_specs=pl.BlockSpec((seq, LANES), lambda b, p: (b, p)),
        out_shape=jax.ShapeDtypeStruct((m, D_MODEL), BF16),
        scratch_shapes=[pltpu.VMEM((LANES, seq), BF16),
                        pltpu.VMEM((seq, LANES), BF16),
                        pltpu.VMEM((kblk, kblk), BF16),
                        pltpu.VMEM((2, qblk, LANES), BF16),
                        pltpu.VMEM((qblk, LANES), F32),
                        pltpu.VMEM((2, qblk, 1), F32)],
        compiler_params=_params("arbitrary", "arbitrary"),
        name="sb_attn_prompt",
    )(bias, q, kt, vt)


def _sb_sample_body(pt_ref, bias_ref, q_ref, kn_ref, vn_ref, *refs, pages_per_step):
    k_refs = refs[:pages_per_step]
    v_refs = refs[pages_per_step:2 * pages_per_step]
    o_ref = refs[2 * pages_per_step]
    qbd_ref, acc_ref, carry_ref, kpad_ref, vpad_ref = refs[2 * pages_per_step + 1:]
    s = pl.program_id(1)
    nq = kn_ref.shape[-1]
    rows = SB_HEADS * SUBLANES
    upper = _minus_from_key_on(PAGE_SIZE)

    def visit(pages, mask):
        zs = [_dot(qbd_ref[...], k3.astype(BF16).reshape(D_MODEL, PAGE_SIZE)) + bias_ref[...]
              for k3, _ in pages]
        (weights, carry), = _stick_break([zs], [mask] * len(pages), [carry_ref[...]], upper)
        acc = acc_ref[...]
        for a, (_, v3) in zip(weights, pages):
            a3 = a.reshape(SB_HEADS, SUBLANES, PAGE_SIZE)
            acc = acc + lax.dot_general(a3, v3, (((2,), (2,)), ((0,), (0,))),
                                        preferred_element_type=F32)
        acc_ref[...] = acc
        carry_ref[...] = carry

    @pl.when(s == 0)
    def _():
        qrep = jnp.concatenate([q_ref[...]] * (D_MODEL // LANES), axis=1)
        row_head = lax.broadcasted_iota(jnp.int32, (rows, D_MODEL), 0) // SUBLANES
        col_head = lax.broadcasted_iota(jnp.int32, (rows, D_MODEL), 1) // SB_HEAD_DIM
        qbd_ref[...] = jnp.where(row_head == col_head, qrep, 0.0).astype(BF16)
        acc_ref[...] = jnp.zeros_like(acc_ref)
        carry_ref[...] = jnp.zeros_like(carry_ref)
        kpad_ref[...] = jnp.zeros_like(kpad_ref)
        vpad_ref[...] = jnp.zeros_like(vpad_ref)
        kpad_ref[:, :, 0:nq] = kn_ref[...]
        vpad_ref[:, :, 0:nq] = vn_ref[...]
        key = lax.broadcasted_iota(jnp.int32, (rows, PAGE_SIZE), 1)
        qry = lax.broadcasted_iota(jnp.int32, (rows, PAGE_SIZE), 0) % SUBLANES
        visit([(kpad_ref[...], vpad_ref[...])], key < qry)

    @pl.when(s > 0)
    def _():
        visit([(k_ref[...], v_ref[...]) for k_ref, v_ref in zip(k_refs, v_refs)], None)

    @pl.when(s == pl.num_programs(1) - 1)
    def _():
        o_ref[...] = acc_ref[...]


def _sb_sample(q, kn, vn, cache_k, cache_v, page_table, bias_col, pages_per_step):
    batch = q.shape[0]
    nq = kn.shape[-1]
    n_pages = page_table.shape[1]
    steps = n_pages // pages_per_step
    rows = SB_HEADS * SUBLANES

    def page_spec(i):
        def index(b, s, pt):
            first = jnp.maximum(s - 1, 0) * pages_per_step
            return (pt[b, n_pages - 1 - (first + i)], 0, 0, 0)
        return pl.BlockSpec((None, SB_HEADS, SB_HEAD_DIM, PAGE_SIZE), index)

    per_seq = lambda b, s, pt: (b, 0, 0, 0)
    new_spec = pl.BlockSpec((None, SB_HEADS, SB_HEAD_DIM, nq), per_seq)
    out_spec = pl.BlockSpec((None, SB_HEADS, SUBLANES, SB_HEAD_DIM), per_seq)
    page_specs = [page_spec(i) for i in range(pages_per_step)]
    grid_spec = pltpu.PrefetchScalarGridSpec(
        num_scalar_prefetch=1,
        grid=(batch, steps + 1),
        in_specs=[pl.BlockSpec((rows, 1), lambda b, s, pt: (0, 0)),
                  pl.BlockSpec((None, rows, LANES), lambda b, s, pt: (b, 0, 0)),
                  new_spec, new_spec]
        + page_specs + page_specs,
        out_specs=out_spec,
        scratch_shapes=[pltpu.VMEM((rows, D_MODEL), BF16),
                        pltpu.VMEM((SB_HEADS, SUBLANES, SB_HEAD_DIM), F32),
                        pltpu.VMEM((rows, 1), F32),
                        pltpu.VMEM((SB_HEADS, SB_HEAD_DIM, PAGE_SIZE), F32),
                        pltpu.VMEM((SB_HEADS, SB_HEAD_DIM, PAGE_SIZE), F32)],
    )
    return pl.pallas_call(
        functools.partial(_sb_sample_body, pages_per_step=pages_per_step),
        grid_spec=grid_spec,
        out_shape=jax.ShapeDtypeStruct((batch, SB_HEADS, SUBLANES, SB_HEAD_DIM), F32),
        compiler_params=_params("arbitrary", "arbitrary"),
        name="sb_attn_sample",
    )(page_table, bias_col, q, kn, vn,
      *([cache_k] * pages_per_step), *([cache_v] * pages_per_step))


def _ffn_body(a_ref, wo_ref, x_ref, g_ref, wgu_ref, wd_ref, o_ref, h_ref, *, hidden, chunk):
    x = x_ref[...] + _dot(a_ref[...].astype(BF16), wo_ref[...])
    h_ref[...] = _rms_rows(x, g_ref[...]).astype(BF16)
    o_ref[...] = x

    def step(ci, _):
        off = pl.multiple_of(ci * chunk, chunk)
        off_up = pl.multiple_of(hidden + ci * chunk, chunk)
        gate = _dot(h_ref[...], wgu_ref[:, pl.ds(off, chunk)])
        up = _dot(h_ref[...], wgu_ref[:, pl.ds(off_up, chunk)])
        act = (gate * jax.nn.sigmoid(gate) * up).astype(BF16)
        o_ref[...] += _dot(act, wd_ref[pl.ds(off, chunk), :])
        return 0

    lax.fori_loop(0, hidden // chunk, step, 0, unroll=True)


def _mix_out_ffn(a, wo, x, gain, wgu, wd, tm, chunk=256):
    m = x.shape[0]
    hidden = wd.shape[0]
    row = lambda i: (i, 0)
    return pl.pallas_call(
        functools.partial(_ffn_body, hidden=hidden, chunk=chunk),
        grid=(m // tm,),
        in_specs=[pl.BlockSpec((tm, D_MODEL), row),
                  _resident((D_MODEL, D_MODEL)),
                  pl.BlockSpec((tm, D_MODEL), row),
                  _resident((1, D_MODEL)),
                  _resident((D_MODEL, 2 * hidden)),
                  _resident((hidden, D_MODEL))],
        out_specs=pl.BlockSpec((tm, D_MODEL), row),
        out_shape=jax.ShapeDtypeStruct((m, D_MODEL), F32),
        scratch_shapes=[pltpu.VMEM((tm, D_MODEL), BF16)],
        compiler_params=_params("arbitrary"),
        name="mix_out_ffn",
    )(a, wo, x, gain, wgu, wd)


ML_QK_W = ML_HEADS * ML_QK_DIM
ML_GATES = 2 * ML_HEADS


def _ml_proj_body(x_ref, g_ref, w_ref, wgt_ref, gb_ref, q_ref, k_ref, v_ref, og_ref, gr_ref,
                  h_ref):
    h_ref[...] = _rms_rows(x_ref[...], g_ref[...]).astype(BF16)
    q_ref[...] = _dot(h_ref[...], w_ref[:, 0:ML_QK_W]).astype(BF16)
    k_ref[...] = (_dot(h_ref[...], w_ref[:, ML_QK_W:2 * ML_QK_W])
                  * (ML_QK_DIM ** -0.5)).astype(BF16)
    v_ref[...] = _dot(h_ref[...], w_ref[:, 2 * ML_QK_W:2 * ML_QK_W + D_MODEL]).astype(BF16)
    og_ref[...] = _dot(h_ref[...], w_ref[:, 2 * ML_QK_W + D_MODEL:2 * ML_QK_W + 2 * D_MODEL])
    g = _dot_nt(wgt_ref[...], h_ref[...]) + gb_ref[...]
    g = ML_GATE_CAP * jnp.tanh(g * (1.0 / ML_GATE_CAP))
    is_input_gate = (lax.broadcasted_iota(jnp.int32, g.shape, 0) % 4) < 2
    gr_ref[...] = jnp.where(is_input_gate, g, _log_sigmoid(g))


def _ml_proj(x, gain, w, wgt, gb, tm):
    m = x.shape[0]
    row = lambda i: (i, 0)
    return pl.pallas_call(
        _ml_proj_body,
        grid=(m // tm,),
        in_specs=[pl.BlockSpec((tm, D_MODEL), row),
                  _resident((1, D_MODEL)),
                  _resident((D_MODEL, 2 * ML_QK_W + 2 * D_MODEL)),
                  _resident((ML_GATES, D_MODEL)),
                  _resident((ML_GATES, 1))],
        out_specs=[pl.BlockSpec((tm, ML_QK_W), row),
                   pl.BlockSpec((tm, ML_QK_W), row),
                   pl.BlockSpec((tm, D_MODEL), row),
                   pl.BlockSpec((tm, D_MODEL), row),
                   pl.BlockSpec((ML_GATES, tm), lambda i: (0, i))],
        out_shape=[jax.ShapeDtypeStruct((m, ML_QK_W), BF16),
                   jax.ShapeDtypeStruct((m, ML_QK_W), BF16),
                   jax.ShapeDtypeStruct((m, D_MODEL), BF16),
                   jax.ShapeDtypeStruct((m, D_MODEL), F32),
                   jax.ShapeDtypeStruct((ML_GATES, m), F32)],
        scratch_shapes=[pltpu.VMEM((tm, D_MODEL), BF16)],
        compiler_params=_params("arbitrary"),
        name="ml_proj",
    )(x, gain, w, wgt, gb)


def _mlstm_body(q_ref, k_ref, v_ref, og_ref, gr_ref, hg_ref, c0_ref, n0_ref, m0_ref,
                y_ref, c_ref, n_ref, m_ref, cn_ref, gate_ref, *, seq, chunk):
    lane = lax.broadcasted_iota(jnp.int32, (1, LANES), 1)
    sub = lax.broadcasted_iota(jnp.int32, (LANES, 1), 0)
    head_lanes = (lane < ML_QK_DIM, lane >= ML_QK_DIM)
    r = lax.broadcasted_iota(jnp.int32, (LANES, LANES), 0)
    c = lax.broadcasted_iota(jnp.int32, (LANES, LANES), 1)
    causal = c <= r
    eye = c == r
    two = lax.broadcasted_iota(jnp.int32, (1, 2), 1)
    pairs = ML_HEADS // 2
    heads = [(p, hd) for p in range(pairs) for hd in range(2)]
    ones = jnp.ones((LANES, LANES), BF16)
    last = slice(LANES - 1, LANES)
    in_chunk = lax.broadcasted_iota(jnp.int32, (2, seq), 1) % chunk

    def chunk_scan(x, op, fill):
        shift = 1
        while shift < chunk:
            x = op(x, jnp.where(in_chunk >= shift, pltpu.roll(x, shift, axis=1), fill))
            shift *= 2
        return x

    m_ref[...] = m0_ref[...]
    for p in range(pairs):
        cn_ref[p, :, 0:ML_V_DIM] = c0_ref[p]
        cn_ref[p, :, ML_V_DIM:] = jnp.broadcast_to(n0_ref[p], (LANES, LANES)).T
        b = chunk_scan(gr_ref[p, 2:4, :], jnp.add, 0.0)
        u = gr_ref[p, 0:2, :] - b
        gate_ref[p, 0:2, :] = b
        gate_ref[p, 2:4, :] = u
        gate_ref[p, 4:6, :] = chunk_scan(u, jnp.maximum, -jnp.inf)

    def step(ci, _):
        rows = pl.ds(pl.multiple_of(ci * chunk, chunk), chunk)
        q = [q_ref[rows, p * LANES:(p + 1) * LANES] for p in range(pairs)]
        k = [k_ref[rows, p * LANES:(p + 1) * LANES] for p in range(pairs)]
        g = [gate_ref[p, :, rows] for p in range(pairs)]
        cn = [cn_ref[p] for p in range(pairs)]
        m_prev = {(p, hd): m_ref[p][:, hd:hd + 1] for p, hd in heads}
        vcols = {(p, hd): slice((2 * p + hd) * ML_V_DIM, (2 * p + hd + 1) * ML_V_DIM)
                 for p, hd in heads}
        qm = {(p, hd): jnp.where(head_lanes[hd], q[p], jnp.zeros_like(q[p])) for p, hd in heads}
        qk = {h: _dot_nt(qm[h], k[h[0]]) for h in heads}
        qcn = {h: _dot(qm[h], cn[h[0]].astype(BF16)) for h in heads}
        b_rows, u2, big2, mt2 = [], [], [], []
        for p in range(pairs):
            prev2 = jnp.concatenate([m_prev[(p, 0)], m_prev[(p, 1)]], axis=0)
            big = jnp.maximum(prev2, g[p][4:6, :])
            b_rows.append(g[p][0:2, :])
            u2.append(g[p][2:4, :])
            big2.append(big)
            mt2.append(g[p][0:2, :] + big)
        cols = [jnp.concatenate([big2[p], mt2[p], jnp.zeros((4, LANES), F32)], axis=0).T
                for p in range(pairs)]
        big_b, s = {}, {}
        for h in heads:
            p, hd = h
            big_b[h] = jnp.broadcast_to(cols[p][:, hd:hd + 1], (LANES, LANES))
            decayed = jnp.where(causal, jnp.exp(u2[p][hd:hd + 1, :] - big_b[h]), 0.0)
            s[h] = (qk[h] * decayed).astype(BF16)
        sv = {h: _dot(s[h], jnp.concatenate([v_ref[rows, vcols[h]], ones], axis=1))
              for h in heads}
        hh = {}
        for h in heads:
            p, hd = h
            inter = jnp.exp(m_prev[h] - big_b[h])
            num = inter * qcn[h][:, :ML_V_DIM] + sv[h][:, :ML_V_DIM]
            den = inter * qcn[h][:, ML_V_DIM:] + sv[h][:, ML_V_DIM:]
            m_t = jnp.broadcast_to(cols[p][:, 2 + hd:3 + hd], (LANES, LANES))
            hh[h] = num / jnp.maximum(jnp.abs(den), jnp.exp(-m_t))
        msq = {h: _dot((hh[h] * hh[h]).astype(BF16), ones) for h in heads}
        for h in heads:
            hn = hh[h] * lax.rsqrt(msq[h] * (1.0 / ML_V_DIM) + RMS_EPS) * hg_ref[:, vcols[h]]
            y_ref[rows, vcols[h]] = (hn * jax.nn.sigmoid(og_ref[rows, vcols[h]])).astype(y_ref.dtype)
        upd, decay, m_new = {}, {}, {}
        first_head = sub < ML_QK_DIM
        for p in range(pairs):
            m_new[p] = mt2[p][:, last]
            b_last = b_rows[p][:, last]
            w2 = jnp.exp(b_last + u2[p] - m_new[p])
            d2 = jnp.exp(b_last + jnp.concatenate([m_prev[(p, 0)], m_prev[(p, 1)]], axis=0)
                         - m_new[p])
            decay[p] = jnp.where(first_head, d2[0:1, :], d2[1:2, :])
            kt = k[p].astype(F32).T
            kwt = jnp.where(first_head, kt * w2[0:1, :], kt * w2[1:2, :]).astype(BF16)
            vext = jnp.concatenate([v_ref[rows, vcols[(p, 0)]], v_ref[rows, vcols[(p, 1)]], ones],
                                   axis=1)
            upd[p] = _dot(kwt, vext)
        for p in range(pairs):
            d_c = jnp.where(first_head, upd[p][:, :ML_V_DIM], upd[p][:, ML_V_DIM:2 * ML_V_DIM])
            cn_ref[p] = decay[p] * cn[p] + jnp.concatenate([d_c, upd[p][:, 2 * ML_V_DIM:]], axis=1)
            m_ref[p] = jnp.where(two == 0, m_new[p][0:1, :], m_new[p][1:2, :])
        return 0

    lax.fori_loop(0, seq // chunk, step, 0)
    for p in range(pairs):
        c_ref[p] = cn_ref[p, :, 0:ML_V_DIM]
        n_ref[p] = jnp.sum(jnp.where(eye, cn_ref[p, :, ML_V_DIM:], 0.0), axis=0, keepdims=True)


def _mlstm(q, k, v, og, gr, head_gain, c0, n0, m0, batch, seq, chunk):
    assert chunk == LANES and seq % chunk == 0
    m = q.shape[0]
    pairs = ML_HEADS // 2
    tok = lambda b: (b, 0)
    st = lambda b: (b, 0, 0, 0)
    states = [pl.BlockSpec((None, pairs, LANES, ML_V_DIM), st),
              pl.BlockSpec((None, pairs, 1, LANES), st),
              pl.BlockSpec((None, pairs, 1, 2), st)]
    return pl.pallas_call(
        functools.partial(_mlstm_body, seq=seq, chunk=chunk),
        grid=(batch,),
        in_specs=[pl.BlockSpec((seq, ML_QK_W), tok),
                  pl.BlockSpec((seq, ML_QK_W), tok),
                  pl.BlockSpec((seq, D_MODEL), tok),
                  pl.BlockSpec((seq, D_MODEL), tok),
                  pl.BlockSpec((pairs, None, 4, seq), lambda b: (0, b, 0, 0)),
                  _resident((1, D_MODEL))] + states,
        out_specs=[pl.BlockSpec((seq, D_MODEL), tok)] + states,
        out_shape=[jax.ShapeDtypeStruct((m, D_MODEL), BF16),
                   jax.ShapeDtypeStruct((batch, pairs, LANES, ML_V_DIM), F32),
                   jax.ShapeDtypeStruct((batch, pairs, 1, LANES), F32),
                   jax.ShapeDtypeStruct((batch, pairs, 1, 2), F32)],
        scratch_shapes=[pltpu.VMEM((pairs, LANES, 2 * ML_V_DIM), F32),
                        pltpu.VMEM((pairs, 6, seq), F32)],
        compiler_params=_params("arbitrary"),
        name="mlstm",
    )(q, k, v, og, gr, head_gain, c0, n0, m0)


SAMPLE_PAD = ML_CHUNK
PAGES_PER_STEP = 16
SB_QUERY_BLOCK = 512
SB_KEY_BLOCK = 256


def _pad_tokens(a, batch, seq, value=0.0):
    a = a.reshape(batch, seq, a.shape[-1])
    a = jnp.pad(a, ((0, 0), (0, SAMPLE_PAD - seq), (0, 0)), constant_values=value)
    return a.reshape(batch * SAMPLE_PAD, a.shape[-1])


def _heads_last(xt, batch, seq):
    return xt.reshape(batch, SB_HEADS, SB_HEAD_DIM, seq).transpose(0, 3, 1, 2)


def kernel(x_prompt, x_sample, cache_k, cache_v, state_C, state_n, state_m, page_table,
           norm_mix, norm_ffn, w_sb_in, sb_q_gain, sb_k_gain, sb_logit_bias, w_sb_out,
           w_ml_in, ml_gate_bias, ml_head_gain, w_ml_out, ffn_w_gate_up, ffn_w_down):
    bp, tp, _ = x_prompt.shape
    bs, ts, _ = x_sample.shape
    depth = norm_mix.shape[0]
    n_phys = cache_k.shape[1]
    xp = x_prompt.reshape(bp * tp, D_MODEL)
    xs = x_sample.reshape(bs * ts, D_MODEL)
    tm_p, tm_s = 512, bs * ts
    ml_pairs = ML_HEADS // 2
    gate_perm = jnp.array([g for p in range(ml_pairs)
                           for g in (2 * p, 2 * p + 1, ML_HEADS + 2 * p, ML_HEADS + 2 * p + 1)])
    ck = cache_k.transpose(0, 1, 3, 4, 2).reshape(-1, SB_HEADS, SB_HEAD_DIM, PAGE_SIZE)
    cv = cache_v.transpose(0, 1, 3, 4, 2).reshape(-1, SB_HEADS, SB_HEAD_DIM, PAGE_SIZE)

    kp_rows, vp_rows, ks_rows, vs_rows = [], [], [], []
    c_p, n_p, m_p, c_s, n_s, m_s = [], [], [], [], [], []
    for i in range(depth):
        j = i // 2
        gain = norm_mix[i].reshape(1, D_MODEL)
        if i % 2 == 0:
            wq = w_sb_in[j][:, :D_MODEL].astype(BF16)
            wkt = w_sb_in[j][:, D_MODEL:2 * D_MODEL].T.astype(BF16)
            wvt = w_sb_in[j][:, 2 * D_MODEL:].T.astype(BF16)
            w_out = w_sb_out[j].astype(BF16)
            qg = jnp.tile(sb_q_gain[j], SB_HEADS).reshape(1, D_MODEL)
            kg = sb_k_gain[j].reshape(1, SB_HEAD_DIM, 1)
            bias = sb_logit_bias[j]
            q, kt, vt = _sb_qkv(xp, gain, wq, wkt, wvt, qg, kg, tm_p, tp)
            mix_p = _sb_attn(q, kt, vt, bias, bp, tp, SB_QUERY_BLOCK, SB_KEY_BLOCK)
            kp_rows.append(_heads_last(kt, bp, tp))
            vp_rows.append(_heads_last(vt, bp, tp))
            q, kt, vt = _sb_qkv(xs, gain, wq, wkt, wvt, qg, kg, tm_s, tm_s)
            q8 = q.astype(F32).reshape(bs, ts, SB_HEADS, SB_HEAD_DIM).transpose(0, 2, 1, 3)
            q8 = jnp.pad(q8, ((0, 0), (0, 0), (0, SUBLANES - ts), (0, 0)))
            q8 = jnp.concatenate([q8, q8], axis=-1).reshape(bs, SB_HEADS * SUBLANES, LANES)
            new = lambda xt: xt.reshape(SB_HEADS, SB_HEAD_DIM, bs, ts).transpose(2, 0, 1, 3)
            bias_col = jnp.repeat(bias, SUBLANES).reshape(SB_HEADS * SUBLANES, 1)
            att = _sb_sample(q8, new(kt), new(vt), ck, cv, page_table + j * n_phys,
                             bias_col, PAGES_PER_STEP)
            mix_s = att[:, :, :ts].transpose(0, 2, 1, 3).reshape(bs * ts, D_MODEL)
            ks_rows.append(_heads_last(kt, 1, bs * ts).reshape(bs, ts, SB_HEADS, SB_HEAD_DIM))
            vs_rows.append(_heads_last(vt, 1, bs * ts).reshape(bs, ts, SB_HEADS, SB_HEAD_DIM))
        else:
            n_main = 2 * ML_QK_W + 2 * D_MODEL
            w_main = w_ml_in[j][:, :n_main].astype(BF16)
            wgt = w_ml_in[j][:, n_main:][:, gate_perm].T.astype(BF16)
            gb = ml_gate_bias[j][gate_perm].reshape(ML_GATES, 1)
            w_out = w_ml_out[j].astype(BF16)
            hg = ml_head_gain[j].reshape(1, D_MODEL)
            q, k, v, og, gr = _ml_proj(xp, gain, w_main, wgt, gb, tm_p)
            zc = jnp.zeros((bp, ml_pairs, LANES, ML_V_DIM), F32)
            zn = jnp.zeros((bp, ml_pairs, 1, LANES), F32)
            zm = jnp.zeros((bp, ml_pairs, 1, 2), F32)
            gr = gr.reshape(ml_pairs, 4, bp, tp).transpose(0, 2, 1, 3)
            mix_p, c1, n1, m1 = _mlstm(q, k, v, og, gr, hg, zc, zn, zm, bp, tp, ML_CHUNK)
            c_p.append(c1.reshape(bp, ML_HEADS, ML_QK_DIM, ML_V_DIM))
            n_p.append(n1.reshape(bp, ML_HEADS, ML_QK_DIM))
            m_p.append(m1.reshape(bp, ML_HEADS))
            q, k, v, og, gr = _ml_proj(xs, gain, w_main, wgt, gb, tm_s)
            gr = gr.reshape(ML_GATES, bs, ts)
            pad_i = jnp.full((ML_GATES, bs, SAMPLE_PAD - ts), -jnp.inf, F32)
            pad_f = jnp.zeros((ML_GATES, bs, SAMPLE_PAD - ts), F32)
            is_input_gate = (jnp.arange(ML_GATES) % 4 < 2)[:, None, None]
            gr = jnp.concatenate([gr, jnp.where(is_input_gate, pad_i, pad_f)], axis=-1)
            gr = gr.reshape(ml_pairs, 4, bs, SAMPLE_PAD).transpose(0, 2, 1, 3)
            y, c2, n2, m2 = _mlstm(
                _pad_tokens(q, bs, ts), _pad_tokens(k, bs, ts), _pad_tokens(v, bs, ts),
                _pad_tokens(og, bs, ts), gr, hg,
                state_C[j].reshape(bs, ml_pairs, LANES, ML_V_DIM),
                state_n[j].reshape(bs, ml_pairs, 1, LANES),
                state_m[j].reshape(bs, ml_pairs, 1, 2), bs, SAMPLE_PAD, SAMPLE_PAD)
            mix_s = y.reshape(bs, SAMPLE_PAD, D_MODEL)[:, :ts].reshape(bs * ts, D_MODEL)
            c_s.append(c2.reshape(bs, ML_HEADS, ML_QK_DIM, ML_V_DIM))
            n_s.append(n2.reshape(bs, ML_HEADS, ML_QK_DIM))
            m_s.append(m2.reshape(bs, ML_HEADS))
        fg = norm_ffn[i].reshape(1, D_MODEL)
        wgu = ffn_w_gate_up[i].astype(BF16)
        wd = ffn_w_down[i].astype(BF16)
        xp = _mix_out_ffn(mix_p, w_out, xp, fg, wgu, wd, tm_p)
        xs = _mix_out_ffn(mix_s, w_out, xs, fg, wgu, wd, tm_s)

    return (xp.reshape(bp, tp, D_MODEL), xs.reshape(bs, ts, D_MODEL),
            jnp.stack(kp_rows), jnp.stack(vp_rows),
            jnp.stack(c_p), jnp.stack(n_p), jnp.stack(m_p),
            jnp.stack(ks_rows), jnp.stack(vs_rows),
            jnp.stack(c_s), jnp.stack(n_s), jnp.stack(m_s))
```

```python
import functools

import jax
import jax.numpy as jnp
from jax import lax
from jax.experimental import pallas as pl
from jax.experimental.pallas import tpu as pltpu

D_MODEL = 1024
SB_HEADS = 16
SB_HEAD_DIM = 64
ML_HEADS = 8
ML_V_DIM = 128
ML_QK_DIM = 64
ML_CHUNK = 128
ML_GATE_CAP = 15.0
RMS_EPS = 1e-6
PAGE_SIZE = 128

LANES = 128
SUBLANES = 8
NORM_GROUP = 256
VMEM_LIMIT = 56 * 1024 * 1024

BF16 = jnp.bfloat16
F32 = jnp.float32


def _params(*sem):
    return pltpu.CompilerParams(dimension_semantics=sem, vmem_limit_bytes=VMEM_LIMIT)


def _resident(shape):
    nd = len(shape)
    return pl.BlockSpec(shape, lambda *_: (0,) * nd, pipeline_mode=pl.Buffered(1))


def _dot(a, b):
    return jnp.dot(a, b, preferred_element_type=F32)


def _dot_nt(a, b):
    return lax.dot_general(a, b, (((1,), (1,)), ((), ())), preferred_element_type=F32)


def _dot_tn(a, b):
    return lax.dot_general(a, b, (((0,), (0,)), ((), ())), preferred_element_type=F32)


def _rms_rows(x, gain_row):
    ms = jnp.mean(x * x, axis=-1, keepdims=True)
    return x * lax.rsqrt(ms + RMS_EPS) * gain_row


def _log_sigmoid(z):
    return jnp.minimum(z, 0.0) - jnp.log(1.0 + jnp.exp(-jnp.abs(z)))


def _minus_from_key_on(n):
    r = lax.broadcasted_iota(jnp.int32, (n, n), 0)
    c = lax.broadcasted_iota(jnp.int32, (n, n), 1)
    return jnp.where(r >= c, -1.0, 0.0).astype(BF16)


LOG2E = 1.4426950408889634
SB_QUERY_SCALE = SB_HEAD_DIM ** -0.5 * LOG2E


def _block_exponents(z, mask, minus_from):
    sign = jnp.uint32(0x80000000)
    neg_abs = lax.bitcast_convert_type(lax.bitcast_convert_type(z, jnp.uint32) | sign, F32)
    sp = jnp.maximum(z, 0.0) + jnp.log(1.0 + jnp.exp2(neg_abs)) * LOG2E
    if mask is not None:
        sp = jnp.where(mask, sp, 0.0)
    tail = _dot(sp.astype(BF16), minus_from)
    expo = z + tail
    if mask is not None:
        expo = jnp.where(mask, expo, -jnp.inf)
    return expo, tail[:, :1]


def _block_weights(blocks, carry):
    weights = []
    for expo, total in blocks:
        weights.append(jnp.exp2(expo + carry))
        carry = carry + total
    return weights, carry


def _sb_qkv_body(x_ref, g_ref, wq_ref, wkt_ref, wvt_ref, qg_ref, kg_ref, q_ref, kt_ref, vt_ref,
                 h_ref):
    h_ref[...] = _rms_rows(x_ref[...], g_ref[...]).astype(BF16)
    r = lax.broadcasted_iota(jnp.int32, (NORM_GROUP, NORM_GROUP), 0) // SB_HEAD_DIM
    c = lax.broadcasted_iota(jnp.int32, (NORM_GROUP, NORM_GROUP), 1) // SB_HEAD_DIM
    same_head = jnp.where(r == c, 1.0, 0.0).astype(BF16)
    tm = h_ref.shape[0]
    groups = [slice(ci * NORM_GROUP, (ci + 1) * NORM_GROUP) for ci in range(D_MODEL // NORM_GROUP)]
    yq = [_dot(h_ref[...], wq_ref[:, cols]) for cols in groups]
    yk = _dot_nt(wkt_ref[...], h_ref[...]).reshape(SB_HEADS, SB_HEAD_DIM, tm)
    vt_ref[...] = _dot_nt(wvt_ref[...], h_ref[...])
    for y, cols in zip(yq, groups):
        ss = _dot((y * y).astype(BF16), same_head)
        qn = y * lax.rsqrt(ss * (1.0 / SB_HEAD_DIM) + RMS_EPS) * qg_ref[:, cols]
        q_ref[:, cols] = (qn * SB_QUERY_SCALE).astype(BF16)
    ss = jnp.mean(yk * yk, axis=1, keepdims=True)
    kt_ref[...] = (yk * lax.rsqrt(ss + RMS_EPS) * kg_ref[...]).reshape(D_MODEL, tm)


def _sb_qkv(x, gain, wq, wkt, wvt, qg, kg, tm, seq):
    m = x.shape[0]
    per_seq = seq // tm
    row = lambda i: (i, 0)
    feat = lambda i: (i // per_seq, 0, i % per_seq)
    return pl.pallas_call(
        _sb_qkv_body,
        grid=(m // tm,),
        in_specs=[pl.BlockSpec((tm, D_MODEL), row),
                  _resident((1, D_MODEL)),
                  _resident((D_MODEL, D_MODEL)),
                  _resident((D_MODEL, D_MODEL)),
                  _resident((D_MODEL, D_MODEL)),
                  _resident((1, D_MODEL)),
                  _resident((1, SB_HEAD_DIM, 1))],
        out_specs=[pl.BlockSpec((tm, D_MODEL), row),
                   pl.BlockSpec((None, D_MODEL, tm), feat),
                   pl.BlockSpec((None, D_MODEL, tm), feat)],
        out_shape=[jax.ShapeDtypeStruct((m, D_MODEL), BF16),
                   jax.ShapeDtypeStruct((m // seq, D_MODEL, seq), F32),
                   jax.ShapeDtypeStruct((m // seq, D_MODEL, seq), F32)],
        scratch_shapes=[pltpu.VMEM((tm, D_MODEL), BF16)],
        compiler_params=_params("arbitrary"),
        name="sb_qkv",
    )(x, gain, wq, wkt, wvt, qg, kg)


def _sb_attn_body(bias_ref, q_ref, kt_ref, vt_ref, o_ref,
                  kb_ref, vb_ref, up_ref, qm_ref, acc_ref, carry_ref, *, seq, qblk, kblk):
    hp = pl.program_id(1)
    lane = lax.broadcasted_iota(jnp.int32, (1, LANES), 1)
    first = lane < SB_HEAD_DIM
    bias = (bias_ref[2 * hp] * LOG2E, bias_ref[2 * hp + 1] * LOG2E)
    kb_ref[...] = kt_ref[...].astype(BF16)
    for t in range(seq // LANES):
        vb_ref[t * LANES:(t + 1) * LANES, :] = (
            vt_ref[:, t * LANES:(t + 1) * LANES].T.astype(BF16))
    up_ref[...] = _minus_from_key_on(kblk)
    r = lax.broadcasted_iota(jnp.int32, (qblk, kblk), 0)
    c = lax.broadcasted_iota(jnp.int32, (qblk, kblk), 1)
    ratio = qblk // kblk
    own_keys = [c + d * kblk < r for d in reversed(range(ratio))]

    def tile(p, masks):
        off = pl.multiple_of(p * qblk, qblk)
        kb = kb_ref[:, pl.ds(off, qblk)]
        vb = vb_ref[pl.ds(off, qblk), :]
        zs = [_dot(qm_ref[hd], kb) + bias[hd] for hd in range(2)]
        blocks = [[_block_exponents(z[:, d * kblk:(d + 1) * kblk], masks[i], up_ref[...])
                   for i, d in enumerate(reversed(range(ratio)))]
                  for z in zs]
        pv = []
        for hd in range(2):
            weights, carry_ref[hd] = _block_weights(blocks[hd], carry_ref[hd])
            pv.append(_dot(jnp.concatenate(weights[::-1], axis=1).astype(BF16), vb))
        acc_ref[...] += jnp.where(first, pv[0], pv[1])

    def query_block(qi, _):
        qoff = pl.multiple_of(qi * qblk, qblk)
        q = q_ref[pl.ds(qoff, qblk), :]
        qm_ref[0] = jnp.where(first, q, jnp.zeros_like(q))
        qm_ref[1] = jnp.where(first, jnp.zeros_like(q), q)
        acc_ref[...] = jnp.zeros_like(acc_ref)
        carry_ref[...] = jnp.zeros_like(carry_ref)
        tile(qi, own_keys)

        def older_tile(t, _):
            tile(qi - 1 - t, [None] * ratio)
            return 0

        lax.fori_loop(0, qi, older_tile, 0)
        o_ref[pl.ds(qoff, qblk), :] = acc_ref[...].astype(o_ref.dtype)
        return 0

    lax.fori_loop(0, seq // qblk, query_block, 0)


def _sb_attn(q, kt, vt, bias, batch, seq, qblk, kblk):
    m = q.shape[0]
    pairs = SB_HEADS // 2
    return pl.pallas_call(
        functools.partial(_sb_attn_body, seq=seq, qblk=qblk, kblk=kblk),
        grid=(batch, pairs),
        in_specs=[pl.BlockSpec(memory_space=pltpu.SMEM),
                  pl.BlockSpec((seq, LANES), lambda b, p: (b, p)),
                  pl.BlockSpec((None, LANES, seq), lambda b, p: (b, p, 0)),
                  pl.BlockSpec((None, LANES, seq), lambda b, p: (b, p, 0))],
        out_specs=pl.BlockSpec((seq, LANES), lambda b, p: (b, p)),
        out_shape=jax.ShapeDtypeStruct((m, D_MODEL), BF16),
        scratch_shapes=[pltpu.VMEM((LANES, seq), BF16),
                        pltpu.VMEM((seq, LANES), BF16),
                        pltpu.VMEM((kblk, kblk), BF16),
                        pltpu.VMEM((2, qblk, LANES), BF16),
                        pltpu.VMEM((qblk, LANES), F32),
                        pltpu.VMEM((2, qblk, 1), F32)],
        compiler_params=_params("arbitrary", "arbitrary"),
        name="sb_attn_prompt",
    )(bias, q, kt, vt)


def _sb_sample_body(pt_ref, bias_ref, q_ref, kn_ref, vn_ref, *refs, pages_per_step):
    k_refs = refs[:pages_per_step]
    v_refs = refs[pages_per_step:2 * pages_per_step]
    o_ref = refs[2 * pages_per_step]
    qbd_ref, acc_ref, carry_ref, kpad_ref, vpad_ref = refs[2 * pages_per_step + 1:]
    s = pl.program_id(1)
    nq = kn_ref.shape[-1]
    rows = SB_HEADS * SUBLANES
    upper = _minus_from_key_on(PAGE_SIZE)

    def visit(pages, mask):
        bias = bias_ref[...] * LOG2E
        zs = [_dot(qbd_ref[...], k3.astype(BF16).reshape(D_MODEL, PAGE_SIZE)) + bias
              for k3, _ in pages]
        blocks = [_block_exponents(z, mask, upper) for z in zs]
        weights, carry = _block_weights(blocks, carry_ref[...])
        acc = acc_ref[...]
        for a, (_, v3) in zip(weights, pages):
            a3 = a.reshape(SB_HEADS, SUBLANES, PAGE_SIZE)
            acc = acc + lax.dot_general(a3, v3, (((2,), (2,)), ((0,), (0,))),
                                        preferred_element_type=F32)
        acc_ref[...] = acc
        carry_ref[...] = carry

    @pl.when(s == 0)
    def _():
        qrep = jnp.concatenate([q_ref[...]] * (D_MODEL // LANES), axis=1)
        row_head = lax.broadcasted_iota(jnp.int32, (rows, D_MODEL), 0) // SUBLANES
        col_head = lax.broadcasted_iota(jnp.int32, (rows, D_MODEL), 1) // SB_HEAD_DIM
        qbd_ref[...] = jnp.where(row_head == col_head, qrep, 0.0).astype(BF16)
        acc_ref[...] = jnp.zeros_like(acc_ref)
        carry_ref[...] = jnp.zeros_like(carry_ref)
        kpad_ref[...] = jnp.zeros_like(kpad_ref)
        vpad_ref[...] = jnp.zeros_like(vpad_ref)
        kpad_ref[:, :, 0:nq] = kn_ref[...]
        vpad_ref[:, :, 0:nq] = vn_ref[...]
        key = lax.broadcasted_iota(jnp.int32, (rows, PAGE_SIZE), 1)
        qry = lax.broadcasted_iota(jnp.int32, (rows, PAGE_SIZE), 0) % SUBLANES
        visit([(kpad_ref[...], vpad_ref[...])], key < qry)

    @pl.when(s > 0)
    def _():
        visit([(k_ref[...], v_ref[...]) for k_ref, v_ref in zip(k_refs, v_refs)], None)

    @pl.when(s == pl.num_programs(1) - 1)
    def _():
        o_ref[...] = acc_ref[...]


def _sb_sample(q, kn, vn, cache_k, cache_v, page_table, bias_col, pages_per_step):
    batch = q.shape[0]
    nq = kn.shape[-1]
    n_pages = page_table.shape[1]
    steps = n_pages // pages_per_step
    rows = SB_HEADS * SUBLANES

    def page_spec(i):
        def index(b, s, pt):
            first = jnp.maximum(s - 1, 0) * pages_per_step
            return (pt[b, n_pages - 1 - (first + i)], 0, 0, 0)
        return pl.BlockSpec((None, SB_HEADS, SB_HEAD_DIM, PAGE_SIZE), index)

    per_seq = lambda b, s, pt: (b, 0, 0, 0)
    new_spec = pl.BlockSpec((None, SB_HEADS, SB_HEAD_DIM, nq), per_seq)
    out_spec = pl.BlockSpec((None, SB_HEADS, SUBLANES, SB_HEAD_DIM), per_seq)
    page_specs = [page_spec(i) for i in range(pages_per_step)]
    grid_spec = pltpu.PrefetchScalarGridSpec(
        num_scalar_prefetch=1,
        grid=(batch, steps + 1),
        in_specs=[pl.BlockSpec((rows, 1), lambda b, s, pt: (0, 0)),
                  pl.BlockSpec((None, rows, LANES), lambda b, s, pt: (b, 0, 0)),
                  new_spec, new_spec]
        + page_specs + page_specs,
        out_specs=out_spec,
        scratch_shapes=[pltpu.VMEM((rows, D_MODEL), BF16),
                        pltpu.VMEM((SB_HEADS, SUBLANES, SB_HEAD_DIM), F32),
                        pltpu.VMEM((rows, 1), F32),
                        pltpu.VMEM((SB_HEADS, SB_HEAD_DIM, PAGE_SIZE), F32),
                        pltpu.VMEM((SB_HEADS, SB_HEAD_DIM, PAGE_SIZE), F32)],
    )
    return pl.pallas_call(
        functools.partial(_sb_sample_body, pages_per_step=pages_per_step),
        grid_spec=grid_spec,
        out_shape=jax.ShapeDtypeStruct((batch, SB_HEADS, SUBLANES, SB_HEAD_DIM), F32),
        compiler_params=_params("arbitrary", "arbitrary"),
        name="sb_attn_sample",
    )(page_table, bias_col, q, kn, vn,
      *([cache_k] * pages_per_step), *([cache_v] * pages_per_step))


def _ffn_body(a_ref, wo_ref, x_ref, g_ref, wgu_ref, wd_ref, o_ref, h_ref, *, hidden, chunk):
    x = x_ref[...] + _dot(a_ref[...].astype(BF16), wo_ref[...])
    h_ref[...] = _rms_rows(x, g_ref[...]).astype(BF16)
    o_ref[...] = x

    def step(ci, _):
        off = pl.multiple_of(ci * chunk, chunk)
        off_up = pl.multiple_of(hidden + ci * chunk, chunk)
        gate = _dot(h_ref[...], wgu_ref[:, pl.ds(off, chunk)])
        up = _dot(h_ref[...], wgu_ref[:, pl.ds(off_up, chunk)])
        act = (gate * jax.nn.sigmoid(gate) * up).astype(BF16)
        o_ref[...] += _dot(act, wd_ref[pl.ds(off, chunk), :])
        return 0

    lax.fori_loop(0, hidden // chunk, step, 0, unroll=True)


def _mix_out_ffn(a, wo, x, gain, wgu, wd, tm, chunk=256):
    m = x.shape[0]
    hidden = wd.shape[0]
    row = lambda i: (i, 0)
    return pl.pallas_call(
        functools.partial(_ffn_body, hidden=hidden, chunk=chunk),
        grid=(m // tm,),
        in_specs=[pl.BlockSpec((tm, D_MODEL), row),
                  _resident((D_MODEL, D_MODEL)),
                  pl.BlockSpec((tm, D_MODEL), row),
                  _resident((1, D_MODEL)),
                  _resident((D_MODEL, 2 * hidden)),
                  _resident((hidden, D_MODEL))],
        out_specs=pl.BlockSpec((tm, D_MODEL), row),
        out_shape=jax.ShapeDtypeStruct((m, D_MODEL), F32),
        scratch_shapes=[pltpu.VMEM((tm, D_MODEL), BF16)],
        compiler_params=_params("arbitrary"),
        name="mix_out_ffn",
    )(a, wo, x, gain, wgu, wd)


ML_QK_W = ML_HEADS * ML_QK_DIM
ML_GATES = 2 * ML_HEADS


def _ml_proj_body(x_ref, g_ref, w_ref, wgt_ref, gb_ref, q_ref, k_ref, v_ref, og_ref, gr_ref,
                  h_ref):
    h_ref[...] = _rms_rows(x_ref[...], g_ref[...]).astype(BF16)
    q_ref[...] = _dot(h_ref[...], w_ref[:, 0:ML_QK_W]).astype(BF16)
    k_ref[...] = (_dot(h_ref[...], w_ref[:, ML_QK_W:2 * ML_QK_W])
                  * (ML_QK_DIM ** -0.5)).astype(BF16)
    v_ref[...] = _dot(h_ref[...], w_ref[:, 2 * ML_QK_W:2 * ML_QK_W + D_MODEL]).astype(BF16)
    og_ref[...] = _dot(h_ref[...], w_ref[:, 2 * ML_QK_W + D_MODEL:2 * ML_QK_W + 2 * D_MODEL])
    g = _dot_nt(wgt_ref[...], h_ref[...]) + gb_ref[...]
    g = ML_GATE_CAP * jnp.tanh(g * (1.0 / ML_GATE_CAP))
    is_input_gate = (lax.broadcasted_iota(jnp.int32, g.shape, 0) % 4) < 2
    gr_ref[...] = jnp.where(is_input_gate, g, _log_sigmoid(g))


def _ml_proj(x, gain, w, wgt, gb, tm):
    m = x.shape[0]
    row = lambda i: (i, 0)
    return pl.pallas_call(
        _ml_proj_body,
        grid=(m // tm,),
        in_specs=[pl.BlockSpec((tm, D_MODEL), row),
                  _resident((1, D_MODEL)),
                  _resident((D_MODEL, 2 * ML_QK_W + 2 * D_MODEL)),
                  _resident((ML_GATES, D_MODEL)),
                  _resident((ML_GATES, 1))],
        out_specs=[pl.BlockSpec((tm, ML_QK_W), row),
                   pl.BlockSpec((tm, ML_QK_W), row),
                   pl.BlockSpec((tm, D_MODEL), row),
                   pl.BlockSpec((tm, D_MODEL), row),
                   pl.BlockSpec((ML_GATES, tm), lambda i: (0, i))],
        out_shape=[jax.ShapeDtypeStruct((m, ML_QK_W), BF16),
                   jax.ShapeDtypeStruct((m, ML_QK_W), BF16),
                   jax.ShapeDtypeStruct((m, D_MODEL), BF16),
                   jax.ShapeDtypeStruct((m, D_MODEL), F32),
                   jax.ShapeDtypeStruct((ML_GATES, m), F32)],
        scratch_shapes=[pltpu.VMEM((tm, D_MODEL), BF16)],
        compiler_params=_params("arbitrary"),
        name="ml_proj",
    )(x, gain, w, wgt, gb)


def _mlstm_body(q_ref, k_ref, v_ref, og_ref, gr_ref, hg_ref, c0_ref, n0_ref, m0_ref,
                y_ref, c_ref, n_ref, m_ref, cn_ref, gate_ref, *, seq, chunk):
    lane = lax.broadcasted_iota(jnp.int32, (1, LANES), 1)
    sub = lax.broadcasted_iota(jnp.int32, (LANES, 1), 0)
    head_lanes = (lane < ML_QK_DIM, lane >= ML_QK_DIM)
    r = lax.broadcasted_iota(jnp.int32, (LANES, LANES), 0)
    c = lax.broadcasted_iota(jnp.int32, (LANES, LANES), 1)
    causal = c <= r
    eye = c == r
    two = lax.broadcasted_iota(jnp.int32, (1, 2), 1)
    pairs = ML_HEADS // 2
    heads = [(p, hd) for p in range(pairs) for hd in range(2)]
    ones = jnp.ones((LANES, LANES), BF16)
    last = slice(LANES - 1, LANES)
    in_chunk = lax.broadcasted_iota(jnp.int32, (2, seq), 1) % chunk

    def chunk_scan(x, op, fill):
        shift = 1
        while shift < chunk:
            x = op(x, jnp.where(in_chunk >= shift, pltpu.roll(x, shift, axis=1), fill))
            shift *= 2
        return x

    m_ref[...] = m0_ref[...]
    for p in range(pairs):
        cn_ref[p, :, 0:ML_V_DIM] = c0_ref[p]
        cn_ref[p, :, ML_V_DIM:] = jnp.broadcast_to(n0_ref[p], (LANES, LANES)).T
        b = chunk_scan(gr_ref[p, 2:4, :], jnp.add, 0.0)
        u = gr_ref[p, 0:2, :] - b
        gate_ref[p, 0:2, :] = b
        gate_ref[p, 2:4, :] = u
        gate_ref[p, 4:6, :] = chunk_scan(u, jnp.maximum, -jnp.inf)

    def step(ci, _):
        rows = pl.ds(pl.multiple_of(ci * chunk, chunk), chunk)
        q = [q_ref[rows, p * LANES:(p + 1) * LANES] for p in range(pairs)]
        k = [k_ref[rows, p * LANES:(p + 1) * LANES] for p in range(pairs)]
        g = [gate_ref[p, :, rows] for p in range(pairs)]
        cn = [cn_ref[p] for p in range(pairs)]
        m_prev = {(p, hd): m_ref[p][:, hd:hd + 1] for p, hd in heads}
        vcols = {(p, hd): slice((2 * p + hd) * ML_V_DIM, (2 * p + hd + 1) * ML_V_DIM)
                 for p, hd in heads}
        qm = {(p, hd): jnp.where(head_lanes[hd], q[p], jnp.zeros_like(q[p])) for p, hd in heads}
        qk = {h: _dot_nt(qm[h], k[h[0]]) for h in heads}
        qcn = {h: _dot(qm[h], cn[h[0]].astype(BF16)) for h in heads}
        b_rows, u2, big2, mt2 = [], [], [], []
        for p in range(pairs):
            prev2 = jnp.concatenate([m_prev[(p, 0)], m_prev[(p, 1)]], axis=0)
            big = jnp.maximum(prev2, g[p][4:6, :])
            b_rows.append(g[p][0:2, :])
            u2.append(g[p][2:4, :])
            big2.append(big)
            mt2.append(g[p][0:2, :] + big)
        cols = [jnp.concatenate([big2[p], mt2[p], jnp.zeros((4, LANES), F32)], axis=0).T
                for p in range(pairs)]
        big_b, s = {}, {}
        for h in heads:
            p, hd = h
            big_b[h] = jnp.broadcast_to(cols[p][:, hd:hd + 1], (LANES, LANES))
            decayed = jnp.where(causal, jnp.exp(u2[p][hd:hd + 1, :] - big_b[h]), 0.0)
            s[h] = (qk[h] * decayed).astype(BF16)
        sv = {h: _dot(s[h], jnp.concatenate([v_ref[rows, vcols[h]], ones], axis=1))
              for h in heads}
        hh = {}
        for h in heads:
            p, hd = h
            inter = jnp.exp(m_prev[h] - big_b[h])
            num = inter * qcn[h][:, :ML_V_DIM] + sv[h][:, :ML_V_DIM]
            den = inter * qcn[h][:, ML_V_DIM:] + sv[h][:, ML_V_DIM:]
            m_t = jnp.broadcast_to(cols[p][:, 2 + hd:3 + hd], (LANES, LANES))
            hh[h] = num / jnp.maximum(jnp.abs(den), jnp.exp(-m_t))
        msq = {h: _dot((hh[h] * hh[h]).astype(BF16), ones) for h in heads}
        for h in heads:
            hn = hh[h] * lax.rsqrt(msq[h] * (1.0 / ML_V_DIM) + RMS_EPS) * hg_ref[:, vcols[h]]
            y_ref[rows, vcols[h]] = (hn * jax.nn.sigmoid(og_ref[rows, vcols[h]])).astype(y_ref.dtype)
        upd, decay, m_new = {}, {}, {}
        first_head = sub < ML_QK_DIM
        for p in range(pairs):
            m_new[p] = mt2[p][:, last]
            b_last = b_rows[p][:, last]
            w2 = jnp.exp(b_last + u2[p] - m_new[p])
            d2 = jnp.exp(b_last + jnp.concatenate([m_prev[(p, 0)], m_prev[(p, 1)]], axis=0)
                         - m_new[p])
            decay[p] = jnp.where(first_head, d2[0:1, :], d2[1:2, :])
            kt = k[p].astype(F32).T
            kwt = jnp.where(first_head, kt * w2[0:1, :], kt * w2[1:2, :]).astype(BF16)
            vext = jnp.concatenate([v_ref[rows, vcols[(p, 0)]], v_ref[rows, vcols[(p, 1)]], ones],
                                   axis=1)
            upd[p] = _dot(kwt, vext)
        for p in range(pairs):
            d_c = jnp.where(first_head, upd[p][:, :ML_V_DIM], upd[p][:, ML_V_DIM:2 * ML_V_DIM])
            cn_ref[p] = decay[p] * cn[p] + jnp.concatenate([d_c, upd[p][:, 2 * ML_V_DIM:]], axis=1)
            m_ref[p] = jnp.where(two == 0, m_new[p][0:1, :], m_new[p][1:2, :])
        return 0

    lax.fori_loop(0, seq // chunk, step, 0)
    for p in range(pairs):
        c_ref[p] = cn_ref[p, :, 0:ML_V_DIM]
        n_ref[p] = jnp.sum(jnp.where(eye, cn_ref[p, :, ML_V_DIM:], 0.0), axis=0, keepdims=True)


def _mlstm(q, k, v, og, gr, head_gain, c0, n0, m0, batch, seq, chunk):
    assert chunk == LANES and seq % chunk == 0
    m = q.shape[0]
    pairs = ML_HEADS // 2
    tok = lambda b: (b, 0)
    st = lambda b: (b, 0, 0, 0)
    states = [pl.BlockSpec((None, pairs, LANES, ML_V_DIM), st),
              pl.BlockSpec((None, pairs, 1, LANES), st),
              pl.BlockSpec((None, pairs, 1, 2), st)]
    return pl.pallas_call(
        functools.partial(_mlstm_body, seq=seq, chunk=chunk),
        grid=(batch,),
        in_specs=[pl.BlockSpec((seq, ML_QK_W), tok),
                  pl.BlockSpec((seq, ML_QK_W), tok),
                  pl.BlockSpec((seq, D_MODEL), tok),
                  pl.BlockSpec((seq, D_MODEL), tok),
                  pl.BlockSpec((pairs, None, 4, seq), lambda b: (0, b, 0, 0)),
                  _resident((1, D_MODEL))] + states,
        out_specs=[pl.BlockSpec((seq, D_MODEL), tok)] + states,
        out_shape=[jax.ShapeDtypeStruct((m, D_MODEL), BF16),
                   jax.ShapeDtypeStruct((batch, pairs, LANES, ML_V_DIM), F32),
                   jax.ShapeDtypeStruct((batch, pairs, 1, LANES), F32),
                   jax.ShapeDtypeStruct((batch, pairs, 1, 2), F32)],
        scratch_shapes=[pltpu.VMEM((pairs, LANES, 2 * ML_V_DIM), F32),
                        pltpu.VMEM((pairs, 6, seq), F32)],
        compiler_params=_params("arbitrary"),
        name="mlstm",
    )(q, k, v, og, gr, head_gain, c0, n0, m0)


SAMPLE_PAD = ML_CHUNK
PAGES_PER_STEP = 16
SB_QUERY_BLOCK = 512
SB_KEY_BLOCK = 256


def _pad_tokens(a, batch, seq, value=0.0):
    a = a.reshape(batch, seq, a.shape[-1])
    a = jnp.pad(a, ((0, 0), (0, SAMPLE_PAD - seq), (0, 0)), constant_values=value)
    return a.reshape(batch * SAMPLE_PAD, a.shape[-1])


def _heads_last(xt, batch, seq):
    return xt.reshape(batch, SB_HEADS, SB_HEAD_DIM, seq).transpose(0, 3, 1, 2)


def kernel(x_prompt, x_sample, cache_k, cache_v, state_C, state_n, state_m, page_table,
           norm_mix, norm_ffn, w_sb_in, sb_q_gain, sb_k_gain, sb_logit_bias, w_sb_out,
           w_ml_in, ml_gate_bias, ml_head_gain, w_ml_out, ffn_w_gate_up, ffn_w_down):
    bp, tp, _ = x_prompt.shape
    bs, ts, _ = x_sample.shape
    depth = norm_mix.shape[0]
    n_phys = cache_k.shape[1]
    xp = x_prompt.reshape(bp * tp, D_MODEL)
    xs = x_sample.reshape(bs * ts, D_MODEL)
    tm_p, tm_s = 512, bs * ts
    ml_pairs = ML_HEADS // 2
    gate_perm = jnp.array([g for p in range(ml_pairs)
                           for g in (2 * p, 2 * p + 1, ML_HEADS + 2 * p, ML_HEADS + 2 * p + 1)])
    ck = cache_k.transpose(0, 1, 3, 4, 2).reshape(-1, SB_HEADS, SB_HEAD_DIM, PAGE_SIZE)
    cv = cache_v.transpose(0, 1, 3, 4, 2).reshape(-1, SB_HEADS, SB_HEAD_DIM, PAGE_SIZE)

    kp_rows, vp_rows, ks_rows, vs_rows = [], [], [], []
    c_p, n_p, m_p, c_s, n_s, m_s = [], [], [], [], [], []
    for i in range(depth):
        j = i // 2
        gain = norm_mix[i].reshape(1, D_MODEL)
        if i % 2 == 0:
            wq = w_sb_in[j][:, :D_MODEL].astype(BF16)
            wkt = w_sb_in[j][:, D_MODEL:2 * D_MODEL].T.astype(BF16)
            wvt = w_sb_in[j][:, 2 * D_MODEL:].T.astype(BF16)
            w_out = w_sb_out[j].astype(BF16)
            qg = jnp.tile(sb_q_gain[j], SB_HEADS).reshape(1, D_MODEL)
            kg = sb_k_gain[j].reshape(1, SB_HEAD_DIM, 1)
            bias = sb_logit_bias[j]
            q, kt, vt = _sb_qkv(xp, gain, wq, wkt, wvt, qg, kg, tm_p, tp)
            mix_p = _sb_attn(q, kt, vt, bias, bp, tp, SB_QUERY_BLOCK, SB_KEY_BLOCK)
            kp_rows.append(_heads_last(kt, bp, tp))
            vp_rows.append(_heads_last(vt, bp, tp))
            q, kt, vt = _sb_qkv(xs, gain, wq, wkt, wvt, qg, kg, tm_s, tm_s)
            q8 = q.astype(F32).reshape(bs, ts, SB_HEADS, SB_HEAD_DIM).transpose(0, 2, 1, 3)
            q8 = jnp.pad(q8, ((0, 0), (0, 0), (0, SUBLANES - ts), (0, 0)))
            q8 = jnp.concatenate([q8, q8], axis=-1).reshape(bs, SB_HEADS * SUBLANES, LANES)
            new = lambda xt: xt.reshape(SB_HEADS, SB_HEAD_DIM, bs, ts).transpose(2, 0, 1, 3)
            bias_col = jnp.repeat(bias, SUBLANES).reshape(SB_HEADS * SUBLANES, 1)
            att = _sb_sample(q8, new(kt), new(vt), ck, cv, page_table + j * n_phys,
                             bias_col, PAGES_PER_STEP)
            mix_s = att[:, :, :ts].transpose(0, 2, 1, 3).reshape(bs * ts, D_MODEL)
            ks_rows.append(_heads_last(kt, 1, bs * ts).reshape(bs, ts, SB_HEADS, SB_HEAD_DIM))
            vs_rows.append(_heads_last(vt, 1, bs * ts).reshape(bs, ts, SB_HEADS, SB_HEAD_DIM))
        else:
            n_main = 2 * ML_QK_W + 2 * D_MODEL
            w_main = w_ml_in[j][:, :n_main].astype(BF16)
            wgt = w_ml_in[j][:, n_main:][:, gate_perm].T.astype(BF16)
            gb = ml_gate_bias[j][gate_perm].reshape(ML_GATES, 1)
            w_out = w_ml_out[j].astype(BF16)
            hg = ml_head_gain[j].reshape(1, D_MODEL)
            q, k, v, og, gr = _ml_proj(xp, gain, w_main, wgt, gb, tm_p)
            zc = jnp.zeros((bp, ml_pairs, LANES, ML_V_DIM), F32)
            zn = jnp.zeros((bp, ml_pairs, 1, LANES), F32)
            zm = jnp.zeros((bp, ml_pairs, 1, 2), F32)
            gr = gr.reshape(ml_pairs, 4, bp, tp).transpose(0, 2, 1, 3)
            mix_p, c1, n1, m1 = _mlstm(q, k, v, og, gr, hg, zc, zn, zm, bp, tp, ML_CHUNK)
            c_p.append(c1.reshape(bp, ML_HEADS, ML_QK_DIM, ML_V_DIM))
            n_p.append(n1.reshape(bp, ML_HEADS, ML_QK_DIM))
            m_p.append(m1.reshape(bp, ML_HEADS))
            q, k, v, og, gr = _ml_proj(xs, gain, w_main, wgt, gb, tm_s)
            gr = gr.reshape(ML_GATES, bs, ts)
            pad_i = jnp.full((ML_GATES, bs, SAMPLE_PAD - ts), -jnp.inf, F32)
            pad_f = jnp.zeros((ML_GATES, bs, SAMPLE_PAD - ts), F32)
            is_input_gate = (jnp.arange(ML_GATES) % 4 < 2)[:, None, None]
            gr = jnp.concatenate([gr, jnp.where(is_input_gate, pad_i, pad_f)], axis=-1)
            gr = gr.reshape(ml_pairs, 4, bs, SAMPLE_PAD).transpose(0, 2, 1, 3)
            y, c2, n2, m2 = _mlstm(
                _pad_tokens(q, bs, ts), _pad_tokens(k, bs, ts), _pad_tokens(v, bs, ts),
                _pad_tokens(og, bs, ts), gr, hg,
                state_C[j].reshape(bs, ml_pairs, LANES, ML_V_DIM),
                state_n[j].reshape(bs, ml_pairs, 1, LANES),
                state_m[j].reshape(bs, ml_pairs, 1, 2), bs, SAMPLE_PAD, SAMPLE_PAD)
            mix_s = y.reshape(bs, SAMPLE_PAD, D_MODEL)[:, :ts].reshape(bs * ts, D_MODEL)
            c_s.append(c2.reshape(bs, ML_HEADS, ML_QK_DIM, ML_V_DIM))
            n_s.append(n2.reshape(bs, ML_HEADS, ML_QK_DIM))
            m_s.append(m2.reshape(bs, ML_HEADS))
        fg = norm_ffn[i].reshape(1, D_MODEL)
        wgu = ffn_w_gate_up[i].astype(BF16)
        wd = ffn_w_down[i].astype(BF16)
        xp = _mix_out_ffn(mix_p, w_out, xp, fg, wgu, wd, tm_p)
        xs = _mix_out_ffn(mix_s, w_out, xs, fg, wgu, wd, tm_s)

    return (xp.reshape(bp, tp, D_MODEL), xs.reshape(bs, ts, D_MODEL),
            jnp.stack(kp_rows), jnp.stack(vp_rows),
            jnp.stack(c_p), jnp.stack(n_p), jnp.stack(m_p),
            jnp.stack(ks_rows), jnp.stack(vs_rows),
            jnp.stack(c_s), jnp.stack(n_s), jnp.stack(m_s))
```

```python
import functools

import jax
import jax.numpy as jnp
from jax import lax
from jax.experimental import pallas as pl
from jax.experimental.pallas import tpu as pltpu

D_MODEL = 1024
SB_HEADS = 16
SB_HEAD_DIM = 64
ML_HEADS = 8
ML_V_DIM = 128
ML_QK_DIM = 64
ML_CHUNK = 128
ML_GATE_CAP = 15.0
RMS_EPS = 1e-6
PAGE_SIZE = 128

LANES = 128
SUBLANES = 8
NORM_GROUP = 256
VMEM_LIMIT = 56 * 1024 * 1024

BF16 = jnp.bfloat16
F32 = jnp.float32


def _params(*sem):
    return pltpu.CompilerParams(dimension_semantics=sem, vmem_limit_bytes=VMEM_LIMIT)


def _resident(shape):
    nd = len(shape)
    return pl.BlockSpec(shape, lambda *_: (0,) * nd, pipeline_mode=pl.Buffered(1))


def _dot(a, b):
    return jnp.dot(a, b, preferred_element_type=F32)


def _dot_nt(a, b):
    return lax.dot_general(a, b, (((1,), (1,)), ((), ())), preferred_element_type=F32)


def _dot_tn(a, b):
    return lax.dot_general(a, b, (((0,), (0,)), ((), ())), preferred_element_type=F32)


def _rms_rows(x, gain_row):
    ms = jnp.mean(x * x, axis=-1, keepdims=True)
    return x * lax.rsqrt(ms + RMS_EPS) * gain_row


def _log_sigmoid(z):
    return jnp.minimum(z, 0.0) - jnp.log(1.0 + jnp.exp(-jnp.abs(z)))


def _minus_from_key_on(n):
    r = lax.broadcasted_iota(jnp.int32, (n, n), 0)
    c = lax.broadcasted_iota(jnp.int32, (n, n), 1)
    return jnp.where(r >= c, -1.0, 0.0).astype(BF16)


LOG2E = 1.4426950408889634
SB_QUERY_SCALE = SB_HEAD_DIM ** -0.5 * LOG2E


def _block_exponents(z, mask, minus_from):
    sign = jnp.uint32(0x80000000)
    neg_abs = lax.bitcast_convert_type(lax.bitcast_convert_type(z, jnp.uint32) | sign, F32)
    sp = jnp.maximum(z, 0.0) + jnp.log(1.0 + jnp.exp2(neg_abs)) * LOG2E
    if mask is not None:
        sp = jnp.where(mask, sp, 0.0)
    tail = _dot(sp.astype(BF16), minus_from)
    expo = z + tail
    if mask is not None:
        expo = jnp.where(mask, expo, -jnp.inf)
    return expo, tail[:, :1]


def _block_weights(blocks, carry):
    weights = []
    for expo, total in blocks:
        weights.append(jnp.exp2(expo + carry))
        carry = carry + total
    return weights, carry


def _sb_qkv_body(x_ref, g_ref, wq_ref, wkt_ref, wvt_ref, qg_ref, kg_ref, q_ref, kt_ref, vt_ref,
                 h_ref):
    h_ref[...] = _rms_rows(x_ref[...], g_ref[...]).astype(BF16)
    r = lax.broadcasted_iota(jnp.int32, (NORM_GROUP, NORM_GROUP), 0) // SB_HEAD_DIM
    c = lax.broadcasted_iota(jnp.int32, (NORM_GROUP, NORM_GROUP), 1) // SB_HEAD_DIM
    same_head = jnp.where(r == c, 1.0, 0.0).astype(BF16)
    tm = h_ref.shape[0]
    groups = [slice(ci * NORM_GROUP, (ci + 1) * NORM_GROUP) for ci in range(D_MODEL // NORM_GROUP)]
    yq = [_dot(h_ref[...], wq_ref[:, cols]) for cols in groups]
    yk = _dot_nt(wkt_ref[...], h_ref[...]).reshape(SB_HEADS, SB_HEAD_DIM, tm)
    vt_ref[...] = _dot_nt(wvt_ref[...], h_ref[...])
    for y, cols in zip(yq, groups):
        ss = _dot((y * y).astype(BF16), same_head)
        qn = y * lax.rsqrt(ss * (1.0 / SB_HEAD_DIM) + RMS_EPS) * qg_ref[:, cols]
        q_ref[:, cols] = (qn * SB_QUERY_SCALE).astype(BF16)
    ss = jnp.mean(yk * yk, axis=1, keepdims=True)
    kt_ref[...] = (yk * lax.rsqrt(ss + RMS_EPS) * kg_ref[...]).reshape(D_MODEL, tm)


def _sb_qkv(x, gain, wq, wkt, wvt, qg, kg, tm, seq):
    m = x.shape[0]
    per_seq = seq // tm
    row = lambda i: (i, 0)
    feat = lambda i: (i // per_seq, 0, i % per_seq)
    return pl.pallas_call(
        _sb_qkv_body,
        grid=(m // tm,),
        in_specs=[pl.BlockSpec((tm, D_MODEL), row),
                  _resident((1, D_MODEL)),
                  _resident((D_MODEL, D_MODEL)),
                  _resident((D_MODEL, D_MODEL)),
                  _resident((D_MODEL, D_MODEL)),
                  _resident((1, D_MODEL)),
                  _resident((1, SB_HEAD_DIM, 1))],
        out_specs=[pl.BlockSpec((tm, D_MODEL), row),
                   pl.BlockSpec((None, D_MODEL, tm), feat),
                   pl.BlockSpec((None, D_MODEL, tm), feat)],
        out_shape=[jax.ShapeDtypeStruct((m, D_MODEL), BF16),
                   jax.ShapeDtypeStruct((m // seq, D_MODEL, seq), F32),
                   jax.ShapeDtypeStruct((m // seq, D_MODEL, seq), F32)],
        scratch_shapes=[pltpu.VMEM((tm, D_MODEL), BF16)],
        compiler_params=_params("arbitrary"),
        name="sb_qkv",
    )(x, gain, wq, wkt, wvt, qg, kg)


def _sb_attn_body(bias_ref, q_ref, kt_ref, vt_ref, o_ref,
                  kb_ref, vb_ref, up_ref, qm_ref, acc_ref, carry_ref, *, seq, qblk, kblk, pairs):
    step = pl.program_id(1)
    lane = lax.broadcasted_iota(jnp.int32, (1, LANES), 1)
    first = lane < SB_HEAD_DIM
    heads = [(pr, hd) for pr in range(pairs) for hd in range(2)]
    bias = {(pr, hd): bias_ref[2 * (step * pairs + pr) + hd] * LOG2E for pr, hd in heads}
    lanes_of = [slice(pr * LANES, (pr + 1) * LANES) for pr in range(pairs)]
    kb_ref[...] = kt_ref[...].astype(BF16)
    for pr in range(pairs):
        for t in range(seq // LANES):
            vb_ref[t * LANES:(t + 1) * LANES, lanes_of[pr]] = (
                vt_ref[lanes_of[pr], t * LANES:(t + 1) * LANES].T.astype(BF16))
    up_ref[...] = _minus_from_key_on(kblk)
    r = lax.broadcasted_iota(jnp.int32, (kblk, kblk), 0)
    c = lax.broadcasted_iota(jnp.int32, (kblk, kblk), 1)
    older = c < r
    ratio = qblk // kblk

    def own_tile(qi):
        base = pl.multiple_of(qi * qblk, qblk)
        zs = {}
        for pr, hd in heads:
            for g in range(ratio):
                qm = qm_ref[2 * pr + hd, g * kblk:(g + 1) * kblk, :]
                kb = kb_ref[lanes_of[pr], pl.ds(base, (g + 1) * kblk)]
                zs[(pr, hd, g)] = _dot(qm, kb) + bias[(pr, hd)]
        blocks = {key: [_block_exponents(z[:, d * kblk:(d + 1) * kblk],
                                         older if d == key[2] else None, up_ref[...])
                        for d in reversed(range(key[2] + 1))]
                  for key, z in zs.items()}
        for g in range(ratio):
            rows = slice(g * kblk, (g + 1) * kblk)
            pv = {}
            for pr, hd in heads:
                weights, carry_ref[2 * pr + hd, rows] = _block_weights(
                    blocks[(pr, hd, g)], jnp.zeros((kblk, 1), F32))
                pv[(pr, hd)] = _dot(jnp.concatenate(weights[::-1], axis=1).astype(BF16),
                                    vb_ref[pl.ds(base, (g + 1) * kblk), lanes_of[pr]])
            for pr in range(pairs):
                acc_ref[pr, rows] = jnp.where(first, pv[(pr, 0)], pv[(pr, 1)])

    def tile(p):
        keys = pl.ds(pl.multiple_of(p * qblk, qblk), qblk)
        zs = {(pr, hd): _dot(qm_ref[2 * pr + hd], kb_ref[lanes_of[pr], keys]) + bias[(pr, hd)]
              for pr, hd in heads}
        blocks = {h: [_block_exponents(zs[h][:, d * kblk:(d + 1) * kblk], None, up_ref[...])
                      for d in reversed(range(ratio))]
                  for h in heads}
        pv = {}
        for pr, hd in heads:
            weights, carry_ref[2 * pr + hd] = _block_weights(blocks[(pr, hd)],
                                                             carry_ref[2 * pr + hd])
            pv[(pr, hd)] = _dot(jnp.concatenate(weights[::-1], axis=1).astype(BF16),
                                vb_ref[keys, lanes_of[pr]])
        for pr in range(pairs):
            acc_ref[pr] += jnp.where(first, pv[(pr, 0)], pv[(pr, 1)])

    def query_block(qi, _):
        rows = pl.ds(pl.multiple_of(qi * qblk, qblk), qblk)
        for pr in range(pairs):
            q = q_ref[rows, lanes_of[pr]]
            qm_ref[2 * pr] = jnp.where(first, q, jnp.zeros_like(q))
            qm_ref[2 * pr + 1] = jnp.where(first, jnp.zeros_like(q), q)
        own_tile(qi)

        def older_tile(t, _):
            tile(qi - 1 - t)
            return 0

        lax.fori_loop(0, qi, older_tile, 0)
        for pr in range(pairs):
            o_ref[rows, lanes_of[pr]] = acc_ref[pr].astype(o_ref.dtype)
        return 0

    lax.fori_loop(0, seq // qblk, query_block, 0)


def _sb_attn(q, kt, vt, bias, batch, seq, qblk, kblk, pairs):
    m = q.shape[0]
    width = pairs * LANES
    steps = SB_HEADS // (2 * pairs)
    return pl.pallas_call(
        functools.partial(_sb_attn_body, seq=seq, qblk=qblk, kblk=kblk, pairs=pairs),
        grid=(batch, steps),
        in_specs=[pl.BlockSpec(memory_space=pltpu.SMEM),
                  pl.BlockSpec((seq, width), lambda b, p: (b, p)),
                  pl.BlockSpec((None, width, seq), lambda b, p: (b, p, 0)),
                  pl.BlockSpec((None, width, seq), lambda b, p: (b, p, 0))],
        out_specs=pl.BlockSpec((seq, width), lambda b, p: (b, p)),
        out_shape=jax.ShapeDtypeStruct((m, D_MODEL), BF16),
        scratch_shapes=[pltpu.VMEM((width, seq), BF16),
                        pltpu.VMEM((seq, width), BF16),
                        pltpu.VMEM((kblk, kblk), BF16),
                        pltpu.VMEM((2 * pairs, qblk, LANES), BF16),
                        pltpu.VMEM((pairs, qblk, LANES), F32),
                        pltpu.VMEM((2 * pairs, qblk, 1), F32)],
        compiler_params=_params("arbitrary", "arbitrary"),
        name="sb_attn_prompt",
    )(bias, q, kt, vt)


def _sb_sample_body(pt_ref, bias_ref, q_ref, kn_ref, vn_ref, *refs, pages_per_step):
    k_refs = refs[:pages_per_step]
    v_refs = refs[pages_per_step:2 * pages_per_step]
    o_ref = refs[2 * pages_per_step]
    qbd_ref, acc_ref, carry_ref, kpad_ref, vpad_ref = refs[2 * pages_per_step + 1:]
    s = pl.program_id(1)
    nq = kn_ref.shape[-1]
    rows = SB_HEADS * SUBLANES
    upper = _minus_from_key_on(PAGE_SIZE)

    def visit(pages, mask):
        bias = bias_ref[...] * LOG2E
        zs = [_dot(qbd_ref[...], k3.astype(BF16).reshape(D_MODEL, PAGE_SIZE)) + bias
              for k3, _ in pages]
        blocks = [_block_exponents(z, mask, upper) for z in zs]
        weights, carry = _block_weights(blocks, carry_ref[...])
        acc = acc_ref[...]
        for a, (_, v3) in zip(weights, pages):
            a3 = a.reshape(SB_HEADS, SUBLANES, PAGE_SIZE)
            acc = acc + lax.dot_general(a3, v3, (((2,), (2,)), ((0,), (0,))),
                                        preferred_element_type=F32)
        acc_ref[...] = acc
        carry_ref[...] = carry

    @pl.when(s == 0)
    def _():
        qrep = jnp.concatenate([q_ref[...]] * (D_MODEL // LANES), axis=1)
        row_head = lax.broadcasted_iota(jnp.int32, (rows, D_MODEL), 0) // SUBLANES
        col_head = lax.broadcasted_iota(jnp.int32, (rows, D_MODEL), 1) // SB_HEAD_DIM
        qbd_ref[...] = jnp.where(row_head == col_head, qrep, 0.0).astype(BF16)
        acc_ref[...] = jnp.zeros_like(acc_ref)
        carry_ref[...] = jnp.zeros_like(carry_ref)
        kpad_ref[...] = jnp.zeros_like(kpad_ref)
        vpad_ref[...] = jnp.zeros_like(vpad_ref)
        kpad_ref[:, :, 0:nq] = kn_ref[...]
        vpad_ref[:, :, 0:nq] = vn_ref[...]
        key = lax.broadcasted_iota(jnp.int32, (rows, PAGE_SIZE), 1)
        qry = lax.broadcasted_iota(jnp.int32, (rows, PAGE_SIZE), 0) % SUBLANES
        visit([(kpad_ref[...], vpad_ref[...])], key < qry)

    @pl.when(s > 0)
    def _():
        visit([(k_ref[...], v_ref[...]) for k_ref, v_ref in zip(k_refs, v_refs)], None)

    @pl.when(s == pl.num_programs(1) - 1)
    def _():
        o_ref[...] = acc_ref[...]


def _sb_sample(q, kn, vn, cache_k, cache_v, page_table, bias_col, pages_per_step):
    batch = q.shape[0]
    nq = kn.shape[-1]
    n_pages = page_table.shape[1]
    steps = n_pages // pages_per_step
    rows = SB_HEADS * SUBLANES

    def page_spec(i):
        def index(b, s, pt):
            first = jnp.maximum(s - 1, 0) * pages_per_step
            return (pt[b, n_pages - 1 - (first + i)], 0, 0, 0)
        return pl.BlockSpec((None, SB_HEADS, SB_HEAD_DIM, PAGE_SIZE), index)

    per_seq = lambda b, s, pt: (b, 0, 0, 0)
    new_spec = pl.BlockSpec((None, SB_HEADS, SB_HEAD_DIM, nq), per_seq)
    out_spec = pl.BlockSpec((None, SB_HEADS, SUBLANES, SB_HEAD_DIM), per_seq)
    page_specs = [page_spec(i) for i in range(pages_per_step)]
    grid_spec = pltpu.PrefetchScalarGridSpec(
        num_scalar_prefetch=1,
        grid=(batch, steps + 1),
        in_specs=[pl.BlockSpec((rows, 1), lambda b, s, pt: (0, 0)),
                  pl.BlockSpec((None, rows, LANES), lambda b, s, pt: (b, 0, 0)),
                  new_spec, new_spec]
        + page_specs + page_specs,
        out_specs=out_spec,
        scratch_shapes=[pltpu.VMEM((rows, D_MODEL), BF16),
                        pltpu.VMEM((SB_HEADS, SUBLANES, SB_HEAD_DIM), F32),
                        pltpu.VMEM((rows, 1), F32),
                        pltpu.VMEM((SB_HEADS, SB_HEAD_DIM, PAGE_SIZE), F32),
                        pltpu.VMEM((SB_HEADS, SB_HEAD_DIM, PAGE_SIZE), F32)],
    )
    return pl.pallas_call(
        functools.partial(_sb_sample_body, pages_per_step=pages_per_step),
        grid_spec=grid_spec,
        out_shape=jax.ShapeDtypeStruct((batch, SB_HEADS, SUBLANES, SB_HEAD_DIM), F32),
        compiler_params=_params("arbitrary", "arbitrary"),
        name="sb_attn_sample",
    )(page_table, bias_col, q, kn, vn,
      *([cache_k] * pages_per_step), *([cache_v] * pages_per_step))


def _ffn_body(a_ref, wo_ref, x_ref, g_ref, wgu_ref, wd_ref, o_ref, h_ref, *, hidden, chunk):
    x = x_ref[...] + _dot(a_ref[...].astype(BF16), wo_ref[...])
    h_ref[...] = _rms_rows(x, g_ref[...]).astype(BF16)
    o_ref[...] = x

    def step(ci, _):
        off = pl.multiple_of(ci * chunk, chunk)
        off_up = pl.multiple_of(hidden + ci * chunk, chunk)
        gate = _dot(h_ref[...], wgu_ref[:, pl.ds(off, chunk)])
        up = _dot(h_ref[...], wgu_ref[:, pl.ds(off_up, chunk)])
        act = (gate * jax.nn.sigmoid(gate) * up).astype(BF16)
        o_ref[...] += _dot(act, wd_ref[pl.ds(off, chunk), :])
        return 0

    lax.fori_loop(0, hidden // chunk, step, 0, unroll=True)


def _mix_out_ffn(a, wo, x, gain, wgu, wd, tm, chunk=256):
    m = x.shape[0]
    hidden = wd.shape[0]
    row = lambda i: (i, 0)
    return pl.pallas_call(
        functools.partial(_ffn_body, hidden=hidden, chunk=chunk),
        grid=(m // tm,),
        in_specs=[pl.BlockSpec((tm, D_MODEL), row),
                  _resident((D_MODEL, D_MODEL)),
                  pl.BlockSpec((tm, D_MODEL), row),
                  _resident((1, D_MODEL)),
                  _resident((D_MODEL, 2 * hidden)),
                  _resident((hidden, D_MODEL))],
        out_specs=pl.BlockSpec((tm, D_MODEL), row),
        out_shape=jax.ShapeDtypeStruct((m, D_MODEL), F32),
        scratch_shapes=[pltpu.VMEM((tm, D_MODEL), BF16)],
        compiler_params=_params("arbitrary"),
        name="mix_out_ffn",
    )(a, wo, x, gain, wgu, wd)


ML_QK_W = ML_HEADS * ML_QK_DIM
ML_GATES = 2 * ML_HEADS


def _ml_proj_body(x_ref, g_ref, w_ref, wgt_ref, gb_ref, q_ref, k_ref, v_ref, og_ref, gr_ref,
                  h_ref):
    h_ref[...] = _rms_rows(x_ref[...], g_ref[...]).astype(BF16)
    q_ref[...] = _dot(h_ref[...], w_ref[:, 0:ML_QK_W]).astype(BF16)
    k_ref[...] = (_dot(h_ref[...], w_ref[:, ML_QK_W:2 * ML_QK_W])
                  * (ML_QK_DIM ** -0.5)).astype(BF16)
    v_ref[...] = _dot(h_ref[...], w_ref[:, 2 * ML_QK_W:2 * ML_QK_W + D_MODEL]).astype(BF16)
    og_ref[...] = _dot(h_ref[...], w_ref[:, 2 * ML_QK_W + D_MODEL:2 * ML_QK_W + 2 * D_MODEL])
    g = _dot_nt(wgt_ref[...], h_ref[...]) + gb_ref[...]
    g = ML_GATE_CAP * jnp.tanh(g * (1.0 / ML_GATE_CAP))
    is_input_gate = (lax.broadcasted_iota(jnp.int32, g.shape, 0) % 4) < 2
    gr_ref[...] = jnp.where(is_input_gate, g, _log_sigmoid(g))


def _ml_proj(x, gain, w, wgt, gb, tm):
    m = x.shape[0]
    row = lambda i: (i, 0)
    return pl.pallas_call(
        _ml_proj_body,
        grid=(m // tm,),
        in_specs=[pl.BlockSpec((tm, D_MODEL), row),
                  _resident((1, D_MODEL)),
                  _resident((D_MODEL, 2 * ML_QK_W + 2 * D_MODEL)),
                  _resident((ML_GATES, D_MODEL)),
                  _resident((ML_GATES, 1))],
        out_specs=[pl.BlockSpec((tm, ML_QK_W), row),
                   pl.BlockSpec((tm, ML_QK_W), row),
                   pl.BlockSpec((tm, D_MODEL), row),
                   pl.BlockSpec((tm, D_MODEL), row),
                   pl.BlockSpec((ML_GATES, tm), lambda i: (0, i))],
        out_shape=[jax.ShapeDtypeStruct((m, ML_QK_W), BF16),
                   jax.ShapeDtypeStruct((m, ML_QK_W), BF16),
                   jax.ShapeDtypeStruct((m, D_MODEL), BF16),
                   jax.ShapeDtypeStruct((m, D_MODEL), F32),
                   jax.ShapeDtypeStruct((ML_GATES, m), F32)],
        scratch_shapes=[pltpu.VMEM((tm, D_MODEL), BF16)],
        compiler_params=_params("arbitrary"),
        name="ml_proj",
    )(x, gain, w, wgt, gb)


def _mlstm_body(q_ref, k_ref, v_ref, og_ref, gr_ref, hg_ref, c0_ref, n0_ref, m0_ref,
                y_ref, c_ref, n_ref, m_ref, cn_ref, gate_ref, *, seq, chunk):
    lane = lax.broadcasted_iota(jnp.int32, (1, LANES), 1)
    sub = lax.broadcasted_iota(jnp.int32, (LANES, 1), 0)
    head_lanes = (lane < ML_QK_DIM, lane >= ML_QK_DIM)
    r = lax.broadcasted_iota(jnp.int32, (LANES, LANES), 0)
    c = lax.broadcasted_iota(jnp.int32, (LANES, LANES), 1)
    causal = c <= r
    eye = c == r
    two = lax.broadcasted_iota(jnp.int32, (1, 2), 1)
    pairs = ML_HEADS // 2
    heads = [(p, hd) for p in range(pairs) for hd in range(2)]
    ones = jnp.ones((LANES, LANES), BF16)
    last = slice(LANES - 1, LANES)
    in_chunk = lax.broadcasted_iota(jnp.int32, (2, seq), 1) % chunk

    def chunk_scan(x, op, fill):
        shift = 1
        while shift < chunk:
            x = op(x, jnp.where(in_chunk >= shift, pltpu.roll(x, shift, axis=1), fill))
            shift *= 2
        return x

    m_ref[...] = m0_ref[...]
    for p in range(pairs):
        cn_ref[p, :, 0:ML_V_DIM] = c0_ref[p]
        cn_ref[p, :, ML_V_DIM:] = jnp.broadcast_to(n0_ref[p], (LANES, LANES)).T
        b = chunk_scan(gr_ref[p, 2:4, :], jnp.add, 0.0)
        u = gr_ref[p, 0:2, :] - b
        gate_ref[p, 0:2, :] = b
        gate_ref[p, 2:4, :] = u
        gate_ref[p, 4:6, :] = chunk_scan(u, jnp.maximum, -jnp.inf)

    def step(ci, _):
        rows = pl.ds(pl.multiple_of(ci * chunk, chunk), chunk)
        q = [q_ref[rows, p * LANES:(p + 1) * LANES] for p in range(pairs)]
        k = [k_ref[rows, p * LANES:(p + 1) * LANES] for p in range(pairs)]
        g = [gate_ref[p, :, rows] for p in range(pairs)]
        cn = [cn_ref[p] for p in range(pairs)]
        m_prev = {(p, hd): m_ref[p][:, hd:hd + 1] for p, hd in heads}
        vcols = {(p, hd): slice((2 * p + hd) * ML_V_DIM, (2 * p + hd + 1) * ML_V_DIM)
                 for p, hd in heads}
        qm = {(p, hd): jnp.where(head_lanes[hd], q[p], jnp.zeros_like(q[p])) for p, hd in heads}
        qk = {h: _dot_nt(qm[h], k[h[0]]) for h in heads}
        qcn = {h: _dot(qm[h], cn[h[0]].astype(BF16)) for h in heads}
        b_rows, u2, big2, mt2 = [], [], [], []
        for p in range(pairs):
            prev2 = jnp.concatenate([m_prev[(p, 0)], m_prev[(p, 1)]], axis=0)
            big = jnp.maximum(prev2, g[p][4:6, :])
            b_rows.append(g[p][0:2, :])
            u2.append(g[p][2:4, :])
            big2.append(big)
            mt2.append(g[p][0:2, :] + big)
        cols = [jnp.concatenate([big2[p], mt2[p], jnp.zeros((4, LANES), F32)], axis=0).T
                for p in range(pairs)]
        big_b, s = {}, {}
        for h in heads:
            p, hd = h
            big_b[h] = jnp.broadcast_to(cols[p][:, hd:hd + 1], (LANES, LANES))
            decayed = jnp.where(causal, jnp.exp(u2[p][hd:hd + 1, :] - big_b[h]), 0.0)
            s[h] = (qk[h] * decayed).astype(BF16)
        sv = {h: _dot(s[h], jnp.concatenate([v_ref[rows, vcols[h]], ones], axis=1))
              for h in heads}
        hh = {}
        for h in heads:
            p, hd = h
            inter = jnp.exp(m_prev[h] - big_b[h])
            num = inter * qcn[h][:, :ML_V_DIM] + sv[h][:, :ML_V_DIM]
            den = inter * qcn[h][:, ML_V_DIM:] + sv[h][:, ML_V_DIM:]
            m_t = jnp.broadcast_to(cols[p][:, 2 + hd:3 + hd], (LANES, LANES))
            hh[h] = num / jnp.maximum(jnp.abs(den), jnp.exp(-m_t))
        msq = {h: _dot((hh[h] * hh[h]).astype(BF16), ones) for h in heads}
        for h in heads:
            hn = hh[h] * lax.rsqrt(msq[h] * (1.0 / ML_V_DIM) + RMS_EPS) * hg_ref[:, vcols[h]]
            y_ref[rows, vcols[h]] = (hn * jax.nn.sigmoid(og_ref[rows, vcols[h]])).astype(y_ref.dtype)
        upd, decay, m_new = {}, {}, {}
        first_head = sub < ML_QK_DIM
        for p in range(pairs):
            m_new[p] = mt2[p][:, last]
            b_last = b_rows[p][:, last]
            w2 = jnp.exp(b_last + u2[p] - m_new[p])
            d2 = jnp.exp(b_last + jnp.concatenate([m_prev[(p, 0)], m_prev[(p, 1)]], axis=0)
                         - m_new[p])
            decay[p] = jnp.where(first_head, d2[0:1, :], d2[1:2, :])
            kt = k[p].astype(F32).T
            kwt = jnp.where(first_head, kt * w2[0:1, :], kt * w2[1:2, :]).astype(BF16)
            vext = jnp.concatenate([v_ref[rows, vcols[(p, 0)]], v_ref[rows, vcols[(p, 1)]], ones],
                                   axis=1)
            upd[p] = _dot(kwt, vext)
        for p in range(pairs):
            d_c = jnp.where(first_head, upd[p][:, :ML_V_DIM], upd[p][:, ML_V_DIM:2 * ML_V_DIM])
            cn_ref[p] = decay[p] * cn[p] + jnp.concatenate([d_c, upd[p][:, 2 * ML_V_DIM:]], axis=1)
            m_ref[p] = jnp.where(two == 0, m_new[p][0:1, :], m_new[p][1:2, :])
        return 0

    lax.fori_loop(0, seq // chunk, step, 0)
    for p in range(pairs):
        c_ref[p] = cn_ref[p, :, 0:ML_V_DIM]
        n_ref[p] = jnp.sum(jnp.where(eye, cn_ref[p, :, ML_V_DIM:], 0.0), axis=0, keepdims=True)


def _mlstm(q, k, v, og, gr, head_gain, c0, n0, m0, batch, seq, chunk):
    assert chunk == LANES and seq % chunk == 0
    m = q.shape[0]
    pairs = ML_HEADS // 2
    tok = lambda b: (b, 0)
    st = lambda b: (b, 0, 0, 0)
    states = [pl.BlockSpec((None, pairs, LANES, ML_V_DIM), st),
              pl.BlockSpec((None, pairs, 1, LANES), st),
              pl.BlockSpec((None, pairs, 1, 2), st)]
    return pl.pallas_call(
        functools.partial(_mlstm_body, seq=seq, chunk=chunk),
        grid=(batch,),
        in_specs=[pl.BlockSpec((seq, ML_QK_W), tok),
                  pl.BlockSpec((seq, ML_QK_W), tok),
                  pl.BlockSpec((seq, D_MODEL), tok),
                  pl.BlockSpec((seq, D_MODEL), tok),
                  pl.BlockSpec((pairs, None, 4, seq), lambda b: (0, b, 0, 0)),
                  _resident((1, D_MODEL))] + states,
        out_specs=[pl.BlockSpec((seq, D_MODEL), tok)] + states,
        out_shape=[jax.ShapeDtypeStruct((m, D_MODEL), BF16),
                   jax.ShapeDtypeStruct((batch, pairs, LANES, ML_V_DIM), F32),
                   jax.ShapeDtypeStruct((batch, pairs, 1, LANES), F32),
                   jax.ShapeDtypeStruct((batch, pairs, 1, 2), F32)],
        scratch_shapes=[pltpu.VMEM((pairs, LANES, 2 * ML_V_DIM), F32),
                        pltpu.VMEM((pairs, 6, seq), F32)],
        compiler_params=_params("arbitrary"),
        name="mlstm",
    )(q, k, v, og, gr, head_gain, c0, n0, m0)


SAMPLE_PAD = ML_CHUNK
PAGES_PER_STEP = 16
SB_QUERY_BLOCK = 512
SB_KEY_BLOCK = 256
SB_PAIRS_PER_STEP = 1


def _pad_tokens(a, batch, seq, value=0.0):
    a = a.reshape(batch, seq, a.shape[-1])
    a = jnp.pad(a, ((0, 0), (0, SAMPLE_PAD - seq), (0, 0)), constant_values=value)
    return a.reshape(batch * SAMPLE_PAD, a.shape[-1])


def _heads_last(xt, batch, seq):
    return xt.reshape(batch, SB_HEADS, SB_HEAD_DIM, seq).transpose(0, 3, 1, 2)


def kernel(x_prompt, x_sample, cache_k, cache_v, state_C, state_n, state_m, page_table,
           norm_mix, norm_ffn, w_sb_in, sb_q_gain, sb_k_gain, sb_logit_bias, w_sb_out,
           w_ml_in, ml_gate_bias, ml_head_gain, w_ml_out, ffn_w_gate_up, ffn_w_down):
    bp, tp, _ = x_prompt.shape
    bs, ts, _ = x_sample.shape
    depth = norm_mix.shape[0]
    n_phys = cache_k.shape[1]
    xp = x_prompt.reshape(bp * tp, D_MODEL)
    xs = x_sample.reshape(bs * ts, D_MODEL)
    tm_p, tm_s = 512, bs * ts
    ml_pairs = ML_HEADS // 2
    gate_perm = jnp.array([g for p in range(ml_pairs)
                           for g in (2 * p, 2 * p + 1, ML_HEADS + 2 * p, ML_HEADS + 2 * p + 1)])
    ck = cache_k.transpose(0, 1, 3, 4, 2).reshape(-1, SB_HEADS, SB_HEAD_DIM, PAGE_SIZE)
    cv = cache_v.transpose(0, 1, 3, 4, 2).reshape(-1, SB_HEADS, SB_HEAD_DIM, PAGE_SIZE)

    kp_rows, vp_rows, ks_rows, vs_rows = [], [], [], []
    c_p, n_p, m_p, c_s, n_s, m_s = [], [], [], [], [], []
    for i in range(depth):
        j = i // 2
        gain = norm_mix[i].reshape(1, D_MODEL)
        if i % 2 == 0:
            wq = w_sb_in[j][:, :D_MODEL].astype(BF16)
            wkt = w_sb_in[j][:, D_MODEL:2 * D_MODEL].T.astype(BF16)
            wvt = w_sb_in[j][:, 2 * D_MODEL:].T.astype(BF16)
            w_out = w_sb_out[j].astype(BF16)
            qg = jnp.tile(sb_q_gain[j], SB_HEADS).reshape(1, D_MODEL)
            kg = sb_k_gain[j].reshape(1, SB_HEAD_DIM, 1)
            bias = sb_logit_bias[j]
            q, kt, vt = _sb_qkv(xp, gain, wq, wkt, wvt, qg, kg, tm_p, tp)
            mix_p = _sb_attn(q, kt, vt, bias, bp, tp, SB_QUERY_BLOCK, SB_KEY_BLOCK, SB_PAIRS_PER_STEP)
            kp_rows.append(_heads_last(kt, bp, tp))
            vp_rows.append(_heads_last(vt, bp, tp))
            q, kt, vt = _sb_qkv(xs, gain, wq, wkt, wvt, qg, kg, tm_s, tm_s)
            q8 = q.astype(F32).reshape(bs, ts, SB_HEADS, SB_HEAD_DIM).transpose(0, 2, 1, 3)
            q8 = jnp.pad(q8, ((0, 0), (0, 0), (0, SUBLANES - ts), (0, 0)))
            q8 = jnp.concatenate([q8, q8], axis=-1).reshape(bs, SB_HEADS * SUBLANES, LANES)
            new = lambda xt: xt.reshape(SB_HEADS, SB_HEAD_DIM, bs, ts).transpose(2, 0, 1, 3)
            bias_col = jnp.repeat(bias, SUBLANES).reshape(SB_HEADS * SUBLANES, 1)
            att = _sb_sample(q8, new(kt), new(vt), ck, cv, page_table + j * n_phys,
                             bias_col, PAGES_PER_STEP)
            mix_s = att[:, :, :ts].transpose(0, 2, 1, 3).reshape(bs * ts, D_MODEL)
            ks_rows.append(_heads_last(kt, 1, bs * ts).reshape(bs, ts, SB_HEADS, SB_HEAD_DIM))
            vs_rows.append(_heads_last(vt, 1, bs * ts).reshape(bs, ts, SB_HEADS, SB_HEAD_DIM))
        else:
            n_main = 2 * ML_QK_W + 2 * D_MODEL
            w_main = w_ml_in[j][:, :n_main].astype(BF16)
            wgt = w_ml_in[j][:, n_main:][:, gate_perm].T.astype(BF16)
            gb = ml_gate_bias[j][gate_perm].reshape(ML_GATES, 1)
            w_out = w_ml_out[j].astype(BF16)
            hg = ml_head_gain[j].reshape(1, D_MODEL)
            q, k, v, og, gr = _ml_proj(xp, gain, w_main, wgt, gb, tm_p)
            zc = jnp.zeros((bp, ml_pairs, LANES, ML_V_DIM), F32)
            zn = jnp.zeros((bp, ml_pairs, 1, LANES), F32)
            zm = jnp.zeros((bp, ml_pairs, 1, 2), F32)
            gr = gr.reshape(ml_pairs, 4, bp, tp).transpose(0, 2, 1, 3)
            mix_p, c1, n1, m1 = _mlstm(q, k, v, og, gr, hg, zc, zn, zm, bp, tp, ML_CHUNK)
            c_p.append(c1.reshape(bp, ML_HEADS, ML_QK_DIM, ML_V_DIM))
            n_p.append(n1.reshape(bp, ML_HEADS, ML_QK_DIM))
            m_p.append(m1.reshape(bp, ML_HEADS))
            q, k, v, og, gr = _ml_proj(xs, gain, w_main, wgt, gb, tm_s)
            gr = gr.reshape(ML_GATES, bs, ts)
            pad_i = jnp.full((ML_GATES, bs, SAMPLE_PAD - ts), -jnp.inf, F32)
            pad_f = jnp.zeros((ML_GATES, bs, SAMPLE_PAD - ts), F32)
            is_input_gate = (jnp.arange(ML_GATES) % 4 < 2)[:, None, None]
            gr = jnp.concatenate([gr, jnp.where(is_input_gate, pad_i, pad_f)], axis=-1)
            gr = gr.reshape(ml_pairs, 4, bs, SAMPLE_PAD).transpose(0, 2, 1, 3)
            y, c2, n2, m2 = _mlstm(
                _pad_tokens(q, bs, ts), _pad_tokens(k, bs, ts), _pad_tokens(v, bs, ts),
                _pad_tokens(og, bs, ts), gr, hg,
                state_C[j].reshape(bs, ml_pairs, LANES, ML_V_DIM),
                state_n[j].reshape(bs, ml_pairs, 1, LANES),
                state_m[j].reshape(bs, ml_pairs, 1, 2), bs, SAMPLE_PAD, SAMPLE_PAD)
            mix_s = y.reshape(bs, SAMPLE_PAD, D_MODEL)[:, :ts].reshape(bs * ts, D_MODEL)
            c_s.append(c2.reshape(bs, ML_HEADS, ML_QK_DIM, ML_V_DIM))
            n_s.append(n2.reshape(bs, ML_HEADS, ML_QK_DIM))
            m_s.append(m2.reshape(bs, ML_HEADS))
        fg = norm_ffn[i].reshape(1, D_MODEL)
        wgu = ffn_w_gate_up[i].astype(BF16)
        wd = ffn_w_down[i].astype(BF16)
        xp = _mix_out_ffn(mix_p, w_out, xp, fg, wgu, wd, tm_p)
        xs = _mix_out_ffn(mix_s, w_out, xs, fg, wgu, wd, tm_s)

    return (xp.reshape(bp, tp, D_MODEL), xs.reshape(bs, ts, D_MODEL),
            jnp.stack(kp_rows), jnp.stack(vp_rows),
            jnp.stack(c_p), jnp.stack(n_p), jnp.stack(m_p),
            jnp.stack(ks_rows), jnp.stack(vs_rows),
            jnp.stack(c_s), jnp.stack(n_s), jnp.stack(m_s))
```

```python
import functools

import jax
import jax.numpy as jnp
from jax import lax
from jax.experimental import pallas as pl
from jax.experimental.pallas import tpu as pltpu

D_MODEL = 1024
SB_HEADS = 16
SB_HEAD_DIM = 64
ML_HEADS = 8
ML_V_DIM = 128
ML_QK_DIM = 64
ML_CHUNK = 128
ML_GATE_CAP = 15.0
RMS_EPS = 1e-6
PAGE_SIZE = 128

LANES = 128
SUBLANES = 8
NORM_GROUP = 256
VMEM_LIMIT = 56 * 1024 * 1024

BF16 = jnp.bfloat16
F32 = jnp.float32


def _params(*sem):
    return pltpu.CompilerParams(dimension_semantics=sem, vmem_limit_bytes=VMEM_LIMIT)


def _resident(shape):
    nd = len(shape)
    return pl.BlockSpec(shape, lambda *_: (0,) * nd, pipeline_mode=pl.Buffered(1))


def _dot(a, b):
    return jnp.dot(a, b, preferred_element_type=F32)


def _dot_nt(a, b):
    return lax.dot_general(a, b, (((1,), (1,)), ((), ())), preferred_element_type=F32)


def _dot_tn(a, b):
    return lax.dot_general(a, b, (((0,), (0,)), ((), ())), preferred_element_type=F32)


def _rms_rows(x, gain_row):
    ms = jnp.mean(x * x, axis=-1, keepdims=True)
    return x * lax.rsqrt(ms + RMS_EPS) * gain_row


def _log_sigmoid(z):
    return jnp.minimum(z, 0.0) - jnp.log(1.0 + jnp.exp(-jnp.abs(z)))


def _minus_from_key_on(n):
    r = lax.broadcasted_iota(jnp.int32, (n, n), 0)
    c = lax.broadcasted_iota(jnp.int32, (n, n), 1)
    return jnp.where(r >= c, -1.0, 0.0).astype(BF16)


LOG2E = 1.4426950408889634
SB_QUERY_SCALE = SB_HEAD_DIM ** -0.5 * LOG2E


def _block_exponents(z, mask, minus_from):
    sign = jnp.uint32(0x80000000)
    neg_abs = lax.bitcast_convert_type(lax.bitcast_convert_type(z, jnp.uint32) | sign, F32)
    sp = jnp.maximum(z, 0.0) + jnp.log(1.0 + jnp.exp2(neg_abs)) * LOG2E
    if mask is not None:
        sp = jnp.where(mask, sp, 0.0)
    tail = _dot(sp.astype(BF16), minus_from)
    expo = z + tail
    if mask is not None:
        expo = jnp.where(mask, expo, -jnp.inf)
    return expo, tail[:, :1]


def _block_weights(blocks, carry):
    weights = []
    for expo, total in blocks:
        weights.append(jnp.exp2(expo + carry))
        carry = carry + total
    return weights, carry


def _sb_qkv_body(x_ref, g_ref, wq_ref, wkt_ref, wvt_ref, qg_ref, kg_ref, q_ref, kt_ref, vt_ref,
                 h_ref):
    h_ref[...] = _rms_rows(x_ref[...], g_ref[...]).astype(BF16)
    r = lax.broadcasted_iota(jnp.int32, (NORM_GROUP, NORM_GROUP), 0) // SB_HEAD_DIM
    c = lax.broadcasted_iota(jnp.int32, (NORM_GROUP, NORM_GROUP), 1) // SB_HEAD_DIM
    same_head = jnp.where(r == c, 1.0, 0.0).astype(BF16)
    tm = h_ref.shape[0]
    groups = [slice(ci * NORM_GROUP, (ci + 1) * NORM_GROUP) for ci in range(D_MODEL // NORM_GROUP)]
    yq = [_dot(h_ref[...], wq_ref[:, cols]) for cols in groups]
    yk = _dot_nt(wkt_ref[...], h_ref[...]).reshape(SB_HEADS, SB_HEAD_DIM, tm)
    vt_ref[...] = _dot_nt(wvt_ref[...], h_ref[...])
    for y, cols in zip(yq, groups):
        ss = _dot((y * y).astype(BF16), same_head)
        qn = y * lax.rsqrt(ss * (1.0 / SB_HEAD_DIM) + RMS_EPS) * qg_ref[:, cols]
        q_ref[:, cols] = (qn * SB_QUERY_SCALE).astype(BF16)
    ss = jnp.mean(yk * yk, axis=1, keepdims=True)
    kt_ref[...] = (yk * lax.rsqrt(ss + RMS_EPS) * kg_ref[...]).reshape(D_MODEL, tm)


def _sb_qkv(x, gain, wq, wkt, wvt, qg, kg, tm, seq):
    m = x.shape[0]
    per_seq = seq // tm
    row = lambda i: (i, 0)
    feat = lambda i: (i // per_seq, 0, i % per_seq)
    return pl.pallas_call(
        _sb_qkv_body,
        grid=(m // tm,),
        in_specs=[pl.BlockSpec((tm, D_MODEL), row),
                  _resident((1, D_MODEL)),
                  _resident((D_MODEL, D_MODEL)),
                  _resident((D_MODEL, D_MODEL)),
                  _resident((D_MODEL, D_MODEL)),
                  _resident((1, D_MODEL)),
                  _resident((1, SB_HEAD_DIM, 1))],
        out_specs=[pl.BlockSpec((tm, D_MODEL), row),
                   pl.BlockSpec((None, D_MODEL, tm), feat),
                   pl.BlockSpec((None, D_MODEL, tm), feat)],
        out_shape=[jax.ShapeDtypeStruct((m, D_MODEL), BF16),
                   jax.ShapeDtypeStruct((m // seq, D_MODEL, seq), F32),
                   jax.ShapeDtypeStruct((m // seq, D_MODEL, seq), F32)],
        scratch_shapes=[pltpu.VMEM((tm, D_MODEL), BF16)],
        compiler_params=_params("arbitrary"),
        name="sb_qkv",
    )(x, gain, wq, wkt, wvt, qg, kg)


def _sb_attn_body(bias_ref, q_ref, kt_ref, vt_ref, o_ref,
                  kb_ref, vb_ref, up_ref, qm_ref, acc_ref, carry_ref, *, seq, qblk, kblk, pairs):
    step = pl.program_id(1)
    lane = lax.broadcasted_iota(jnp.int32, (1, LANES), 1)
    first = lane < SB_HEAD_DIM
    heads = [(pr, hd) for pr in range(pairs) for hd in range(2)]
    bias = {(pr, hd): bias_ref[2 * (step * pairs + pr) + hd] * LOG2E for pr, hd in heads}
    lanes_of = [slice(pr * LANES, (pr + 1) * LANES) for pr in range(pairs)]
    kb_ref[...] = kt_ref[...].astype(BF16)
    for pr in range(pairs):
        for t in range(seq // LANES):
            vb_ref[t * LANES:(t + 1) * LANES, lanes_of[pr]] = (
                vt_ref[lanes_of[pr], t * LANES:(t + 1) * LANES].T.astype(BF16))
    up_ref[...] = _minus_from_key_on(kblk)
    r = lax.broadcasted_iota(jnp.int32, (kblk, kblk), 0)
    c = lax.broadcasted_iota(jnp.int32, (kblk, kblk), 1)
    older = c < r
    ratio = qblk // kblk

    def own_tile(qi):
        base = pl.multiple_of(qi * qblk, qblk)
        zs = {}
        for pr, hd in heads:
            for g in range(ratio):
                qm = qm_ref[2 * pr + hd, g * kblk:(g + 1) * kblk, :]
                kb = kb_ref[lanes_of[pr], pl.ds(base, (g + 1) * kblk)]
                zs[(pr, hd, g)] = _dot(qm, kb) + bias[(pr, hd)]
        blocks = {key: [_block_exponents(z[:, d * kblk:(d + 1) * kblk],
                                         older if d == key[2] else None, up_ref[...])
                        for d in reversed(range(key[2] + 1))]
                  for key, z in zs.items()}
        for g in range(ratio):
            rows = slice(g * kblk, (g + 1) * kblk)
            pv = {}
            for pr, hd in heads:
                weights, carry_ref[2 * pr + hd, rows] = _block_weights(
                    blocks[(pr, hd, g)], jnp.zeros((kblk, 1), F32))
                pv[(pr, hd)] = _dot(jnp.concatenate(weights[::-1], axis=1).astype(BF16),
                                    vb_ref[pl.ds(base, (g + 1) * kblk), lanes_of[pr]])
            for pr in range(pairs):
                acc_ref[pr, rows] = jnp.where(first, pv[(pr, 0)], pv[(pr, 1)])

    def tile(p):
        keys = pl.ds(pl.multiple_of(p * qblk, qblk), qblk)
        zs = {(pr, hd): _dot(qm_ref[2 * pr + hd], kb_ref[lanes_of[pr], keys]) + bias[(pr, hd)]
              for pr, hd in heads}
        blocks = {h: [_block_exponents(zs[h][:, d * kblk:(d + 1) * kblk], None, up_ref[...])
                      for d in reversed(range(ratio))]
                  for h in heads}
        pv = {}
        for pr, hd in heads:
            weights, carry_ref[2 * pr + hd] = _block_weights(blocks[(pr, hd)],
                                                             carry_ref[2 * pr + hd])
            pv[(pr, hd)] = _dot(jnp.concatenate([w.astype(BF16) for w in weights[::-1]], axis=1),
                                vb_ref[keys, lanes_of[pr]])
        for pr in range(pairs):
            acc_ref[pr] += jnp.where(first, pv[(pr, 0)], pv[(pr, 1)])

    def query_block(qi, _):
        rows = pl.ds(pl.multiple_of(qi * qblk, qblk), qblk)
        for pr in range(pairs):
            q = q_ref[rows, lanes_of[pr]]
            qm_ref[2 * pr] = jnp.where(first, q, jnp.zeros_like(q))
            qm_ref[2 * pr + 1] = jnp.where(first, jnp.zeros_like(q), q)
        own_tile(qi)

        def older_tile(t, _):
            tile(qi - 1 - t)
            return 0

        lax.fori_loop(0, qi, older_tile, 0)
        for pr in range(pairs):
            o_ref[rows, lanes_of[pr]] = acc_ref[pr].astype(o_ref.dtype)
        return 0

    lax.fori_loop(0, seq // qblk, query_block, 0)


def _sb_attn(q, kt, vt, bias, batch, seq, qblk, kblk, pairs):
    m = q.shape[0]
    width = pairs * LANES
    steps = SB_HEADS // (2 * pairs)
    return pl.pallas_call(
        functools.partial(_sb_attn_body, seq=seq, qblk=qblk, kblk=kblk, pairs=pairs),
        grid=(batch, steps),
        in_specs=[pl.BlockSpec(memory_space=pltpu.SMEM),
                  pl.BlockSpec((seq, width), lambda b, p: (b, p)),
                  pl.BlockSpec((None, width, seq), lambda b, p: (b, p, 0)),
                  pl.BlockSpec((None, width, seq), lambda b, p: (b, p, 0))],
        out_specs=pl.BlockSpec((seq, width), lambda b, p: (b, p)),
        out_shape=jax.ShapeDtypeStruct((m, D_MODEL), BF16),
        scratch_shapes=[pltpu.VMEM((width, seq), BF16),
                        pltpu.VMEM((seq, width), BF16),
                        pltpu.VMEM((kblk, kblk), BF16),
                        pltpu.VMEM((2 * pairs, qblk, LANES), BF16),
                        pltpu.VMEM((pairs, qblk, LANES), F32),
                        pltpu.VMEM((2 * pairs, qblk, 1), F32)],
        compiler_params=_params("arbitrary", "arbitrary"),
        name="sb_attn_prompt",
    )(bias, q, kt, vt)


def _sb_sample_body(pt_ref, bias_ref, q_ref, kn_ref, vn_ref, *refs, pages_per_step):
    k_refs = refs[:pages_per_step]
    v_refs = refs[pages_per_step:2 * pages_per_step]
    o_ref = refs[2 * pages_per_step]
    qbd_ref, acc_ref, carry_ref, kpad_ref, vpad_ref = refs[2 * pages_per_step + 1:]
    s = pl.program_id(1)
    nq = kn_ref.shape[-1]
    rows = SB_HEADS * SUBLANES
    upper = _minus_from_key_on(PAGE_SIZE)

    def visit(pages, mask):
        bias = bias_ref[...] * LOG2E
        zs = [_dot(qbd_ref[...], k3.astype(BF16).reshape(D_MODEL, PAGE_SIZE)) + bias
              for k3, _ in pages]
        blocks = [_block_exponents(z, mask, upper) for z in zs]
        weights, carry = _block_weights(blocks, carry_ref[...])
        acc = acc_ref[...]
        for a, (_, v3) in zip(weights, pages):
            a3 = a.reshape(SB_HEADS, SUBLANES, PAGE_SIZE)
            acc = acc + lax.dot_general(a3, v3, (((2,), (2,)), ((0,), (0,))),
                                        preferred_element_type=F32)
        acc_ref[...] = acc
        carry_ref[...] = carry

    @pl.when(s == 0)
    def _():
        qrep = jnp.concatenate([q_ref[...]] * (D_MODEL // LANES), axis=1)
        row_head = lax.broadcasted_iota(jnp.int32, (rows, D_MODEL), 0) // SUBLANES
        col_head = lax.broadcasted_iota(jnp.int32, (rows, D_MODEL), 1) // SB_HEAD_DIM
        qbd_ref[...] = jnp.where(row_head == col_head, qrep, 0.0).astype(BF16)
        acc_ref[...] = jnp.zeros_like(acc_ref)
        carry_ref[...] = jnp.zeros_like(carry_ref)
        kpad_ref[...] = jnp.zeros_like(kpad_ref)
        vpad_ref[...] = jnp.zeros_like(vpad_ref)
        kpad_ref[:, :, 0:nq] = kn_ref[...]
        vpad_ref[:, :, 0:nq] = vn_ref[...]
        key = lax.broadcasted_iota(jnp.int32, (rows, PAGE_SIZE), 1)
        qry = lax.broadcasted_iota(jnp.int32, (rows, PAGE_SIZE), 0) % SUBLANES
        visit([(kpad_ref[...], vpad_ref[...])], key < qry)

    @pl.when(s > 0)
    def _():
        visit([(k_ref[...], v_ref[...]) for k_ref, v_ref in zip(k_refs, v_refs)], None)

    @pl.when(s == pl.num_programs(1) - 1)
    def _():
        o_ref[...] = acc_ref[...]


def _sb_sample(q, kn, vn, cache_k, cache_v, page_table, bias_col, pages_per_step):
    batch = q.shape[0]
    nq = kn.shape[-1]
    n_pages = page_table.shape[1]
    steps = n_pages // pages_per_step
    rows = SB_HEADS * SUBLANES

    def page_spec(i):
        def index(b, s, pt):
            first = jnp.maximum(s - 1, 0) * pages_per_step
            return (pt[b, n_pages - 1 - (first + i)], 0, 0, 0)
        return pl.BlockSpec((None, SB_HEADS, SB_HEAD_DIM, PAGE_SIZE), index)

    per_seq = lambda b, s, pt: (b, 0, 0, 0)
    new_spec = pl.BlockSpec((None, SB_HEADS, SB_HEAD_DIM, nq), per_seq)
    out_spec = pl.BlockSpec((None, SB_HEADS, SUBLANES, SB_HEAD_DIM), per_seq)
    page_specs = [page_spec(i) for i in range(pages_per_step)]
    grid_spec = pltpu.PrefetchScalarGridSpec(
        num_scalar_prefetch=1,
        grid=(batch, steps + 1),
        in_specs=[pl.BlockSpec((rows, 1), lambda b, s, pt: (0, 0)),
                  pl.BlockSpec((None, rows, LANES), lambda b, s, pt: (b, 0, 0)),
                  new_spec, new_spec]
        + page_specs + page_specs,
        out_specs=out_spec,
        scratch_shapes=[pltpu.VMEM((rows, D_MODEL), BF16),
                        pltpu.VMEM((SB_HEADS, SUBLANES, SB_HEAD_DIM), F32),
                        pltpu.VMEM((rows, 1), F32),
                        pltpu.VMEM((SB_HEADS, SB_HEAD_DIM, PAGE_SIZE), F32),
                        pltpu.VMEM((SB_HEADS, SB_HEAD_DIM, PAGE_SIZE), F32)],
    )
    return pl.pallas_call(
        functools.partial(_sb_sample_body, pages_per_step=pages_per_step),
        grid_spec=grid_spec,
        out_shape=jax.ShapeDtypeStruct((batch, SB_HEADS, SUBLANES, SB_HEAD_DIM), F32),
        compiler_params=_params("arbitrary", "arbitrary"),
        name="sb_attn_sample",
    )(page_table, bias_col, q, kn, vn,
      *([cache_k] * pages_per_step), *([cache_v] * pages_per_step))


def _ffn_body(a_ref, wo_ref, x_ref, g_ref, wgu_ref, wd_ref, o_ref, h_ref, *, hidden, chunk):
    x = x_ref[...] + _dot(a_ref[...].astype(BF16), wo_ref[...])
    h_ref[...] = _rms_rows(x, g_ref[...]).astype(BF16)
    o_ref[...] = x

    def step(ci, _):
        off = pl.multiple_of(ci * chunk, chunk)
        off_up = pl.multiple_of(hidden + ci * chunk, chunk)
        gate = _dot(h_ref[...], wgu_ref[:, pl.ds(off, chunk)])
        up = _dot(h_ref[...], wgu_ref[:, pl.ds(off_up, chunk)])
        act = (gate * jax.nn.sigmoid(gate) * up).astype(BF16)
        o_ref[...] += _dot(act, wd_ref[pl.ds(off, chunk), :])
        return 0

    lax.fori_loop(0, hidden // chunk, step, 0, unroll=True)


def _mix_out_ffn(a, wo, x, gain, wgu, wd, tm, chunk=256):
    m = x.shape[0]
    hidden = wd.shape[0]
    row = lambda i: (i, 0)
    return pl.pallas_call(
        functools.partial(_ffn_body, hidden=hidden, chunk=chunk),
        grid=(m // tm,),
        in_specs=[pl.BlockSpec((tm, D_MODEL), row),
                  _resident((D_MODEL, D_MODEL)),
                  pl.BlockSpec((tm, D_MODEL), row),
                  _resident((1, D_MODEL)),
                  _resident((D_MODEL, 2 * hidden)),
                  _resident((hidden, D_MODEL))],
        out_specs=pl.BlockSpec((tm, D_MODEL), row),
        out_shape=jax.ShapeDtypeStruct((m, D_MODEL), F32),
        scratch_shapes=[pltpu.VMEM((tm, D_MODEL), BF16)],
        compiler_params=_params("arbitrary"),
        name="mix_out_ffn",
    )(a, wo, x, gain, wgu, wd)


ML_QK_W = ML_HEADS * ML_QK_DIM
ML_GATES = 2 * ML_HEADS


def _ml_proj_body(x_ref, g_ref, w_ref, wgt_ref, gb_ref, q_ref, k_ref, v_ref, og_ref, gr_ref,
                  h_ref):
    h_ref[...] = _rms_rows(x_ref[...], g_ref[...]).astype(BF16)
    q_ref[...] = _dot(h_ref[...], w_ref[:, 0:ML_QK_W]).astype(BF16)
    k_ref[...] = (_dot(h_ref[...], w_ref[:, ML_QK_W:2 * ML_QK_W])
                  * (ML_QK_DIM ** -0.5)).astype(BF16)
    v_ref[...] = _dot(h_ref[...], w_ref[:, 2 * ML_QK_W:2 * ML_QK_W + D_MODEL]).astype(BF16)
    og_ref[...] = _dot(h_ref[...], w_ref[:, 2 * ML_QK_W + D_MODEL:2 * ML_QK_W + 2 * D_MODEL])
    g = _dot_nt(wgt_ref[...], h_ref[...]) + gb_ref[...]
    g = ML_GATE_CAP * jnp.tanh(g * (1.0 / ML_GATE_CAP))
    is_input_gate = (lax.broadcasted_iota(jnp.int32, g.shape, 0) % 4) < 2
    gr_ref[...] = jnp.where(is_input_gate, g, _log_sigmoid(g))


def _ml_proj(x, gain, w, wgt, gb, tm):
    m = x.shape[0]
    row = lambda i: (i, 0)
    return pl.pallas_call(
        _ml_proj_body,
        grid=(m // tm,),
        in_specs=[pl.BlockSpec((tm, D_MODEL), row),
                  _resident((1, D_MODEL)),
                  _resident((D_MODEL, 2 * ML_QK_W + 2 * D_MODEL)),
                  _resident((ML_GATES, D_MODEL)),
                  _resident((ML_GATES, 1))],
        out_specs=[pl.BlockSpec((tm, ML_QK_W), row),
                   pl.BlockSpec((tm, ML_QK_W), row),
                   pl.BlockSpec((tm, D_MODEL), row),
                   pl.BlockSpec((tm, D_MODEL), row),
                   pl.BlockSpec((ML_GATES, tm), lambda i: (0, i))],
        out_shape=[jax.ShapeDtypeStruct((m, ML_QK_W), BF16),
                   jax.ShapeDtypeStruct((m, ML_QK_W), BF16),
                   jax.ShapeDtypeStruct((m, D_MODEL), BF16),
                   jax.ShapeDtypeStruct((m, D_MODEL), F32),
                   jax.ShapeDtypeStruct((ML_GATES, m), F32)],
        scratch_shapes=[pltpu.VMEM((tm, D_MODEL), BF16)],
        compiler_params=_params("arbitrary"),
        name="ml_proj",
    )(x, gain, w, wgt, gb)


def _mlstm_body(q_ref, k_ref, v_ref, og_ref, gr_ref, hg_ref, c0_ref, n0_ref, m0_ref,
                y_ref, c_ref, n_ref, m_ref, cn_ref, gate_ref, *, seq, chunk):
    lane = lax.broadcasted_iota(jnp.int32, (1, LANES), 1)
    sub = lax.broadcasted_iota(jnp.int32, (LANES, 1), 0)
    head_lanes = (lane < ML_QK_DIM, lane >= ML_QK_DIM)
    r = lax.broadcasted_iota(jnp.int32, (LANES, LANES), 0)
    c = lax.broadcasted_iota(jnp.int32, (LANES, LANES), 1)
    causal = c <= r
    eye = c == r
    two = lax.broadcasted_iota(jnp.int32, (1, 2), 1)
    pairs = ML_HEADS // 2
    heads = [(p, hd) for p in range(pairs) for hd in range(2)]
    ones = jnp.ones((LANES, LANES), BF16)
    last = slice(LANES - 1, LANES)
    in_chunk = lax.broadcasted_iota(jnp.int32, (2, seq), 1) % chunk

    def chunk_scan(x, op, fill):
        shift = 1
        while shift < chunk:
            x = op(x, jnp.where(in_chunk >= shift, pltpu.roll(x, shift, axis=1), fill))
            shift *= 2
        return x

    m_ref[...] = m0_ref[...]
    for p in range(pairs):
        cn_ref[p, :, 0:ML_V_DIM] = c0_ref[p]
        cn_ref[p, :, ML_V_DIM:] = jnp.broadcast_to(n0_ref[p], (LANES, LANES)).T
        b = chunk_scan(gr_ref[p, 2:4, :], jnp.add, 0.0)
        u = gr_ref[p, 0:2, :] - b
        gate_ref[p, 0:2, :] = b
        gate_ref[p, 2:4, :] = u
        gate_ref[p, 4:6, :] = chunk_scan(u, jnp.maximum, -jnp.inf)

    def step(ci, _):
        rows = pl.ds(pl.multiple_of(ci * chunk, chunk), chunk)
        q = [q_ref[rows, p * LANES:(p + 1) * LANES] for p in range(pairs)]
        k = [k_ref[rows, p * LANES:(p + 1) * LANES] for p in range(pairs)]
        g = [gate_ref[p, :, rows] for p in range(pairs)]
        cn = [cn_ref[p] for p in range(pairs)]
        m_prev = {(p, hd): m_ref[p][:, hd:hd + 1] for p, hd in heads}
        vcols = {(p, hd): slice((2 * p + hd) * ML_V_DIM, (2 * p + hd + 1) * ML_V_DIM)
                 for p, hd in heads}
        qm = {(p, hd): jnp.where(head_lanes[hd], q[p], jnp.zeros_like(q[p])) for p, hd in heads}
        qk = {h: _dot_nt(qm[h], k[h[0]]) for h in heads}
        qcn = {h: _dot(qm[h], cn[h[0]].astype(BF16)) for h in heads}
        b_rows, u2, big2, mt2 = [], [], [], []
        for p in range(pairs):
            prev2 = jnp.concatenate([m_prev[(p, 0)], m_prev[(p, 1)]], axis=0)
            big = jnp.maximum(prev2, g[p][4:6, :])
            b_rows.append(g[p][0:2, :])
            u2.append(g[p][2:4, :])
            big2.append(big)
            mt2.append(g[p][0:2, :] + big)
        cols = [jnp.concatenate([big2[p], mt2[p], jnp.zeros((4, LANES), F32)], axis=0).T
                for p in range(pairs)]
        big_b, s = {}, {}
        for h in heads:
            p, hd = h
            big_b[h] = jnp.broadcast_to(cols[p][:, hd:hd + 1], (LANES, LANES))
            decayed = jnp.where(causal, jnp.exp(u2[p][hd:hd + 1, :] - big_b[h]), 0.0)
            s[h] = (qk[h] * decayed).astype(BF16)
        sv = {h: _dot(s[h], jnp.concatenate([v_ref[rows, vcols[h]], ones], axis=1))
              for h in heads}
        hh = {}
        for h in heads:
            p, hd = h
            inter = jnp.exp(m_prev[h] - big_b[h])
            num = inter * qcn[h][:, :ML_V_DIM] + sv[h][:, :ML_V_DIM]
            den = inter * qcn[h][:, ML_V_DIM:] + sv[h][:, ML_V_DIM:]
            m_t = jnp.broadcast_to(cols[p][:, 2 + hd:3 + hd], (LANES, LANES))
            hh[h] = num / jnp.maximum(jnp.abs(den), jnp.exp(-m_t))
        msq = {h: _dot((hh[h] * hh[h]).astype(BF16), ones) for h in heads}
        for h in heads:
            hn = hh[h] * lax.rsqrt(msq[h] * (1.0 / ML_V_DIM) + RMS_EPS) * hg_ref[:, vcols[h]]
            y_ref[rows, vcols[h]] = (hn * jax.nn.sigmoid(og_ref[rows, vcols[h]])).astype(y_ref.dtype)
        upd, decay, m_new = {}, {}, {}
        first_head = sub < ML_QK_DIM
        for p in range(pairs):
            m_new[p] = mt2[p][:, last]
            b_last = b_rows[p][:, last]
            w2 = jnp.exp(b_last + u2[p] - m_new[p])
            d2 = jnp.exp(b_last + jnp.concatenate([m_prev[(p, 0)], m_prev[(p, 1)]], axis=0)
                         - m_new[p])
            decay[p] = jnp.where(first_head, d2[0:1, :], d2[1:2, :])
            kt = k[p].astype(F32).T
            kwt = jnp.where(first_head, kt * w2[0:1, :], kt * w2[1:2, :]).astype(BF16)
            vext = jnp.concatenate([v_ref[rows, vcols[(p, 0)]], v_ref[rows, vcols[(p, 1)]], ones],
                                   axis=1)
            upd[p] = _dot(kwt, vext)
        for p in range(pairs):
            d_c = jnp.where(first_head, upd[p][:, :ML_V_DIM], upd[p][:, ML_V_DIM:2 * ML_V_DIM])
            cn_ref[p] = decay[p] * cn[p] + jnp.concatenate([d_c, upd[p][:, 2 * ML_V_DIM:]], axis=1)
            m_ref[p] = jnp.where(two == 0, m_new[p][0:1, :], m_new[p][1:2, :])
        return 0

    lax.fori_loop(0, seq // chunk, step, 0)
    for p in range(pairs):
        c_ref[p] = cn_ref[p, :, 0:ML_V_DIM]
        n_ref[p] = jnp.sum(jnp.where(eye, cn_ref[p, :, ML_V_DIM:], 0.0), axis=0, keepdims=True)


def _mlstm(q, k, v, og, gr, head_gain, c0, n0, m0, batch, seq, chunk):
    assert chunk == LANES and seq % chunk == 0
    m = q.shape[0]
    pairs = ML_HEADS // 2
    tok = lambda b: (b, 0)
    st = lambda b: (b, 0, 0, 0)
    states = [pl.BlockSpec((None, pairs, LANES, ML_V_DIM), st),
              pl.BlockSpec((None, pairs, 1, LANES), st),
              pl.BlockSpec((None, pairs, 1, 2), st)]
    return pl.pallas_call(
        functools.partial(_mlstm_body, seq=seq, chunk=chunk),
        grid=(batch,),
        in_specs=[pl.BlockSpec((seq, ML_QK_W), tok),
                  pl.BlockSpec((seq, ML_QK_W), tok),
                  pl.BlockSpec((seq, D_MODEL), tok),
                  pl.BlockSpec((seq, D_MODEL), tok),
                  pl.BlockSpec((pairs, None, 4, seq), lambda b: (0, b, 0, 0)),
                  _resident((1, D_MODEL))] + states,
        out_specs=[pl.BlockSpec((seq, D_MODEL), tok)] + states,
        out_shape=[jax.ShapeDtypeStruct((m, D_MODEL), BF16),
                   jax.ShapeDtypeStruct((batch, pairs, LANES, ML_V_DIM), F32),
                   jax.ShapeDtypeStruct((batch, pairs, 1, LANES), F32),
                   jax.ShapeDtypeStruct((batch, pairs, 1, 2), F32)],
        scratch_shapes=[pltpu.VMEM((pairs, LANES, 2 * ML_V_DIM), F32),
                        pltpu.VMEM((pairs, 6, seq), F32)],
        compiler_params=_params("arbitrary"),
        name="mlstm",
    )(q, k, v, og, gr, head_gain, c0, n0, m0)


SAMPLE_PAD = ML_CHUNK
PAGES_PER_STEP = 16
SB_QUERY_BLOCK = 1024
SB_KEY_BLOCK = 256
SB_PAIRS_PER_STEP = 1


def _pad_tokens(a, batch, seq, value=0.0):
    a = a.reshape(batch, seq, a.shape[-1])
    a = jnp.pad(a, ((0, 0), (0, SAMPLE_PAD - seq), (0, 0)), constant_values=value)
    return a.reshape(batch * SAMPLE_PAD, a.shape[-1])


def _heads_last(xt, batch, seq):
    return xt.reshape(batch, SB_HEADS, SB_HEAD_DIM, seq).transpose(0, 3, 1, 2)


def kernel(x_prompt, x_sample, cache_k, cache_v, state_C, state_n, state_m, page_table,
           norm_mix, norm_ffn, w_sb_in, sb_q_gain, sb_k_gain, sb_logit_bias, w_sb_out,
           w_ml_in, ml_gate_bias, ml_head_gain, w_ml_out, ffn_w_gate_up, ffn_w_down):
    bp, tp, _ = x_prompt.shape
    bs, ts, _ = x_sample.shape
    depth = norm_mix.shape[0]
    n_phys = cache_k.shape[1]
    xp = x_prompt.reshape(bp * tp, D_MODEL)
    xs = x_sample.reshape(bs * ts, D_MODEL)
    tm_p, tm_s = 512, bs * ts
    ml_pairs = ML_HEADS // 2
    gate_perm = jnp.array([g for p in range(ml_pairs)
                           for g in (2 * p, 2 * p + 1, ML_HEADS + 2 * p, ML_HEADS + 2 * p + 1)])
    ck = cache_k.transpose(0, 1, 3, 4, 2).reshape(-1, SB_HEADS, SB_HEAD_DIM, PAGE_SIZE)
    cv = cache_v.transpose(0, 1, 3, 4, 2).reshape(-1, SB_HEADS, SB_HEAD_DIM, PAGE_SIZE)

    kp_rows, vp_rows, ks_rows, vs_rows = [], [], [], []
    c_p, n_p, m_p, c_s, n_s, m_s = [], [], [], [], [], []
    for i in range(depth):
        j = i // 2
        gain = norm_mix[i].reshape(1, D_MODEL)
        if i % 2 == 0:
            wq = w_sb_in[j][:, :D_MODEL].astype(BF16)
            wkt = w_sb_in[j][:, D_MODEL:2 * D_MODEL].T.astype(BF16)
            wvt = w_sb_in[j][:, 2 * D_MODEL:].T.astype(BF16)
            w_out = w_sb_out[j].astype(BF16)
            qg = jnp.tile(sb_q_gain[j], SB_HEADS).reshape(1, D_MODEL)
            kg = sb_k_gain[j].reshape(1, SB_HEAD_DIM, 1)
            bias = sb_logit_bias[j]
            q, kt, vt = _sb_qkv(xp, gain, wq, wkt, wvt, qg, kg, tm_p, tp)
            mix_p = _sb_attn(q, kt, vt, bias, bp, tp, SB_QUERY_BLOCK, SB_KEY_BLOCK, SB_PAIRS_PER_STEP)
            kp_rows.append(_heads_last(kt, bp, tp))
            vp_rows.append(_heads_last(vt, bp, tp))
            q, kt, vt = _sb_qkv(xs, gain, wq, wkt, wvt, qg, kg, tm_s, tm_s)
            q8 = q.astype(F32).reshape(bs, ts, SB_HEADS, SB_HEAD_DIM).transpose(0, 2, 1, 3)
            q8 = jnp.pad(q8, ((0, 0), (0, 0), (0, SUBLANES - ts), (0, 0)))
            q8 = jnp.concatenate([q8, q8], axis=-1).reshape(bs, SB_HEADS * SUBLANES, LANES)
            new = lambda xt: xt.reshape(SB_HEADS, SB_HEAD_DIM, bs, ts).transpose(2, 0, 1, 3)
            bias_col = jnp.repeat(bias, SUBLANES).reshape(SB_HEADS * SUBLANES, 1)
            att = _sb_sample(q8, new(kt), new(vt), ck, cv, page_table + j * n_phys,
                             bias_col, PAGES_PER_STEP)
            mix_s = att[:, :, :ts].transpose(0, 2, 1, 3).reshape(bs * ts, D_MODEL)
            ks_rows.append(_heads_last(kt, 1, bs * ts).reshape(bs, ts, SB_HEADS, SB_HEAD_DIM))
            vs_rows.append(_heads_last(vt, 1, bs * ts).reshape(bs, ts, SB_HEADS, SB_HEAD_DIM))
        else:
            n_main = 2 * ML_QK_W + 2 * D_MODEL
            w_main = w_ml_in[j][:, :n_main].astype(BF16)
            wgt = w_ml_in[j][:, n_main:][:, gate_perm].T.astype(BF16)
            gb = ml_gate_bias[j][gate_perm].reshape(ML_GATES, 1)
            w_out = w_ml_out[j].astype(BF16)
            hg = ml_head_gain[j].reshape(1, D_MODEL)
            q, k, v, og, gr = _ml_proj(xp, gain, w_main, wgt, gb, tm_p)
            zc = jnp.zeros((bp, ml_pairs, LANES, ML_V_DIM), F32)
            zn = jnp.zeros((bp, ml_pairs, 1, LANES), F32)
            zm = jnp.zeros((bp, ml_pairs, 1, 2), F32)
            gr = gr.reshape(ml_pairs, 4, bp, tp).transpose(0, 2, 1, 3)
            mix_p, c1, n1, m1 = _mlstm(q, k, v, og, gr, hg, zc, zn, zm, bp, tp, ML_CHUNK)
            c_p.append(c1.reshape(bp, ML_HEADS, ML_QK_DIM, ML_V_DIM))
            n_p.append(n1.reshape(bp, ML_HEADS, ML_QK_DIM))
            m_p.append(m1.reshape(bp, ML_HEADS))
            q, k, v, og, gr = _ml_proj(xs, gain, w_main, wgt, gb, tm_s)
            gr = gr.reshape(ML_GATES, bs, ts)
            pad_i = jnp.full((ML_GATES, bs, SAMPLE_PAD - ts), -jnp.inf, F32)
            pad_f = jnp.zeros((ML_GATES, bs, SAMPLE_PAD - ts), F32)
            is_input_gate = (jnp.arange(ML_GATES) % 4 < 2)[:, None, None]
            gr = jnp.concatenate([gr, jnp.where(is_input_gate, pad_i, pad_f)], axis=-1)
            gr = gr.reshape(ml_pairs, 4, bs, SAMPLE_PAD).transpose(0, 2, 1, 3)
            y, c2, n2, m2 = _mlstm(
                _pad_tokens(q, bs, ts), _pad_tokens(k, bs, ts), _pad_tokens(v, bs, ts),
                _pad_tokens(og, bs, ts), gr, hg,
                state_C[j].reshape(bs, ml_pairs, LANES, ML_V_DIM),
                state_n[j].reshape(bs, ml_pairs, 1, LANES),
                state_m[j].reshape(bs, ml_pairs, 1, 2), bs, SAMPLE_PAD, SAMPLE_PAD)
            mix_s = y.reshape(bs, SAMPLE_PAD, D_MODEL)[:, :ts].reshape(bs * ts, D_MODEL)
            c_s.append(c2.reshape(bs, ML_HEADS, ML_QK_DIM, ML_V_DIM))
            n_s.append(n2.reshape(bs, ML_HEADS, ML_QK_DIM))
            m_s.append(m2.reshape(bs, ML_HEADS))
        fg = norm_ffn[i].reshape(1, D_MODEL)
        wgu = ffn_w_gate_up[i].astype(BF16)
        wd = ffn_w_down[i].astype(BF16)
        xp = _mix_out_ffn(mix_p, w_out, xp, fg, wgu, wd, tm_p)
        xs = _mix_out_ffn(mix_s, w_out, xs, fg, wgu, wd, tm_s)

    return (xp.reshape(bp, tp, D_MODEL), xs.reshape(bs, ts, D_MODEL),
            jnp.stack(kp_rows), jnp.stack(vp_rows),
            jnp.stack(c_p), jnp.stack(n_p), jnp.stack(m_p),
            jnp.stack(ks_rows), jnp.stack(vs_rows),
            jnp.stack(c_s), jnp.stack(n_s), jnp.stack(m_s))
```

```python
import functools

import jax
import jax.numpy as jnp
from jax import lax
from jax.experimental import pallas as pl
from jax.experimental.pallas import tpu as pltpu

D_MODEL = 1024
SB_HEADS = 16
SB_HEAD_DIM = 64
ML_HEADS = 8
ML_V_DIM = 128
ML_QK_DIM = 64
ML_CHUNK = 128
ML_GATE_CAP = 15.0
RMS_EPS = 1e-6
PAGE_SIZE = 128

LANES = 128
SUBLANES = 8
NORM_GROUP = 256
VMEM_LIMIT = 56 * 1024 * 1024

BF16 = jnp.bfloat16
F32 = jnp.float32


def _params(*sem):
    return pltpu.CompilerParams(dimension_semantics=sem, vmem_limit_bytes=VMEM_LIMIT)


def _resident(shape):
    nd = len(shape)
    return pl.BlockSpec(shape, lambda *_: (0,) * nd, pipeline_mode=pl.Buffered(1))


def _dot(a, b):
    return jnp.dot(a, b, preferred_element_type=F32)


def _dot_nt(a, b):
    return lax.dot_general(a, b, (((1,), (1,)), ((), ())), preferred_element_type=F32)


def _rms_rows(x, gain_row):
    ms = jnp.mean(x * x, axis=-1, keepdims=True)
    return x * lax.rsqrt(ms + RMS_EPS) * gain_row


def _log_sigmoid(z):
    return jnp.minimum(z, 0.0) - jnp.log(1.0 + jnp.exp(-jnp.abs(z)))


def _minus_from_key_on(n):
    r = lax.broadcasted_iota(jnp.int32, (n, n), 0)
    c = lax.broadcasted_iota(jnp.int32, (n, n), 1)
    return jnp.where(r >= c, -1.0, 0.0).astype(BF16)


LOG2E = 1.4426950408889634
SB_QUERY_SCALE = SB_HEAD_DIM ** -0.5 * LOG2E


def _block_exponents(z, mask, minus_from):
    sp = jnp.maximum(z, 0.0) + jnp.log(1.0 + jnp.exp2(-jnp.abs(z))) * LOG2E
    if mask is not None:
        sp = jnp.where(mask, sp, 0.0)
    tail = _dot(sp.astype(BF16), minus_from)
    expo = z + tail
    if mask is not None:
        expo = jnp.where(mask, expo, -jnp.inf)
    return expo, tail[:, :1]


def _block_weights(blocks, carry):
    weights = []
    for expo, total in blocks:
        weights.append(jnp.exp2(expo + carry))
        carry = carry + total
    return weights, carry


def _sb_qkv_body(x_ref, g_ref, wq_ref, wkt_ref, wvt_ref, qg_ref, kg_ref, q_ref, kt_ref, vt_ref,
                 h_ref):
    h_ref[...] = _rms_rows(x_ref[...], g_ref[...]).astype(BF16)
    r = lax.broadcasted_iota(jnp.int32, (NORM_GROUP, NORM_GROUP), 0) // SB_HEAD_DIM
    c = lax.broadcasted_iota(jnp.int32, (NORM_GROUP, NORM_GROUP), 1) // SB_HEAD_DIM
    same_head = jnp.where(r == c, 1.0, 0.0).astype(BF16)
    tm = h_ref.shape[0]
    groups = [slice(ci * NORM_GROUP, (ci + 1) * NORM_GROUP) for ci in range(D_MODEL // NORM_GROUP)]
    yq = [_dot(h_ref[...], wq_ref[:, cols]) for cols in groups]
    yk = _dot_nt(wkt_ref[...], h_ref[...]).reshape(SB_HEADS, SB_HEAD_DIM, tm)
    vt_ref[...] = _dot_nt(wvt_ref[...], h_ref[...])
    for y, cols in zip(yq, groups):
        ss = _dot((y * y).astype(BF16), same_head)
        qn = y * lax.rsqrt(ss * (1.0 / SB_HEAD_DIM) + RMS_EPS) * qg_ref[:, cols]
        q_ref[:, cols] = (qn * SB_QUERY_SCALE).astype(BF16)
    ss = jnp.mean(yk * yk, axis=1, keepdims=True)
    kt_ref[...] = (yk * lax.rsqrt(ss + RMS_EPS) * kg_ref[...]).reshape(D_MODEL, tm)


def _sb_qkv(x, gain, wq, wkt, wvt, qg, kg, tm, seq):
    m = x.shape[0]
    per_seq = seq // tm
    row = lambda i: (i, 0)
    feat = lambda i: (i // per_seq, 0, i % per_seq)
    return pl.pallas_call(
        _sb_qkv_body,
        grid=(m // tm,),
        in_specs=[pl.BlockSpec((tm, D_MODEL), row),
                  _resident((1, D_MODEL)),
                  _resident((D_MODEL, D_MODEL)),
                  _resident((D_MODEL, D_MODEL)),
                  _resident((D_MODEL, D_MODEL)),
                  _resident((1, D_MODEL)),
                  _resident((1, SB_HEAD_DIM, 1))],
        out_specs=[pl.BlockSpec((tm, D_MODEL), row),
                   pl.BlockSpec((None, D_MODEL, tm), feat),
                   pl.BlockSpec((None, D_MODEL, tm), feat)],
        out_shape=[jax.ShapeDtypeStruct((m, D_MODEL), BF16),
                   jax.ShapeDtypeStruct((m // seq, D_MODEL, seq), F32),
                   jax.ShapeDtypeStruct((m // seq, D_MODEL, seq), F32)],
        scratch_shapes=[pltpu.VMEM((tm, D_MODEL), BF16)],
        compiler_params=_params("arbitrary"),
        name="sb_qkv",
    )(x, gain, wq, wkt, wvt, qg, kg)


def _sb_attn_body(bias_ref, q_ref, kt_ref, vt_ref, o_ref,
                  kb_ref, vb_ref, up_ref, qm_ref, acc_ref, carry_ref, *, seq, qblk, kblk, pairs):
    step = pl.program_id(1)
    lane = lax.broadcasted_iota(jnp.int32, (1, LANES), 1)
    first = lane < SB_HEAD_DIM
    heads = [(pr, hd) for pr in range(pairs) for hd in range(2)]
    bias = {(pr, hd): bias_ref[2 * (step * pairs + pr) + hd] * LOG2E for pr, hd in heads}
    lanes_of = [slice(pr * LANES, (pr + 1) * LANES) for pr in range(pairs)]
    kb_ref[...] = kt_ref[...].astype(BF16)
    for pr in range(pairs):
        for t in range(seq // LANES):
            vb_ref[t * LANES:(t + 1) * LANES, lanes_of[pr]] = (
                vt_ref[lanes_of[pr], t * LANES:(t + 1) * LANES].T.astype(BF16))
    up_ref[...] = _minus_from_key_on(kblk)
    r = lax.broadcasted_iota(jnp.int32, (kblk, kblk), 0)
    c = lax.broadcasted_iota(jnp.int32, (kblk, kblk), 1)
    older = c < r
    ratio = qblk // kblk

    def own_tile(qi):
        base = pl.multiple_of(qi * qblk, qblk)
        zs = {}
        for pr, hd in heads:
            for g in range(ratio):
                qm = qm_ref[2 * pr + hd, g * kblk:(g + 1) * kblk, :]
                kb = kb_ref[lanes_of[pr], pl.ds(base, (g + 1) * kblk)]
                zs[(pr, hd, g)] = _dot(qm, kb) + bias[(pr, hd)]
        blocks = {key: [_block_exponents(z[:, d * kblk:(d + 1) * kblk],
                                         older if d == key[2] else None, up_ref[...])
                        for d in reversed(range(key[2] + 1))]
                  for key, z in zs.items()}
        for g in range(ratio):
            rows = slice(g * kblk, (g + 1) * kblk)
            pv = {}
            for pr, hd in heads:
                weights, carry_ref[2 * pr + hd, rows] = _block_weights(
                    blocks[(pr, hd, g)], jnp.zeros((kblk, 1), F32))
                pv[(pr, hd)] = _dot(jnp.concatenate(weights[::-1], axis=1).astype(BF16),
                                    vb_ref[pl.ds(base, (g + 1) * kblk), lanes_of[pr]])
            for pr in range(pairs):
                acc_ref[pr, rows] = jnp.where(first, pv[(pr, 0)], pv[(pr, 1)])

    def tile(p):
        keys = pl.ds(pl.multiple_of(p * qblk, qblk), qblk)
        zs = {(pr, hd): _dot(qm_ref[2 * pr + hd], kb_ref[lanes_of[pr], keys]) + bias[(pr, hd)]
              for pr, hd in heads}
        blocks = {h: [_block_exponents(zs[h][:, d * kblk:(d + 1) * kblk], None, up_ref[...])
                      for d in reversed(range(ratio))]
                  for h in heads}
        pv = {}
        for pr, hd in heads:
            weights, carry_ref[2 * pr + hd] = _block_weights(blocks[(pr, hd)],
                                                             carry_ref[2 * pr + hd])
            pv[(pr, hd)] = _dot(jnp.concatenate([w.astype(BF16) for w in weights[::-1]], axis=1),
                                vb_ref[keys, lanes_of[pr]])
        for pr in range(pairs):
            acc_ref[pr] += jnp.where(first, pv[(pr, 0)], pv[(pr, 1)])

    def query_block(qi, _):
        rows = pl.ds(pl.multiple_of(qi * qblk, qblk), qblk)
        for pr in range(pairs):
            q = q_ref[rows, lanes_of[pr]]
            qm_ref[2 * pr] = jnp.where(first, q, jnp.zeros_like(q))
            qm_ref[2 * pr + 1] = jnp.where(first, jnp.zeros_like(q), q)
        own_tile(qi)

        def older_tile(t, _):
            tile(qi - 1 - t)
            return 0

        lax.fori_loop(0, qi, older_tile, 0)
        for pr in range(pairs):
            o_ref[rows, lanes_of[pr]] = acc_ref[pr].astype(o_ref.dtype)
        return 0

    lax.fori_loop(0, seq // qblk, query_block, 0)


def _sb_attn(q, kt, vt, bias, batch, seq, qblk, kblk, pairs):
    m = q.shape[0]
    width = pairs * LANES
    steps = SB_HEADS // (2 * pairs)
    return pl.pallas_call(
        functools.partial(_sb_attn_body, seq=seq, qblk=qblk, kblk=kblk, pairs=pairs),
        grid=(batch, steps),
        in_specs=[pl.BlockSpec(memory_space=pltpu.SMEM),
                  pl.BlockSpec((seq, width), lambda b, p: (b, p)),
                  pl.BlockSpec((None, width, seq), lambda b, p: (b, p, 0)),
                  pl.BlockSpec((None, width, seq), lambda b, p: (b, p, 0))],
        out_specs=pl.BlockSpec((seq, width), lambda b, p: (b, p)),
        out_shape=jax.ShapeDtypeStruct((m, D_MODEL), BF16),
        scratch_shapes=[pltpu.VMEM((width, seq), BF16),
                        pltpu.VMEM((seq, width), BF16),
                        pltpu.VMEM((kblk, kblk), BF16),
                        pltpu.VMEM((2 * pairs, qblk, LANES), BF16),
                        pltpu.VMEM((pairs, qblk, LANES), F32),
                        pltpu.VMEM((2 * pairs, qblk, 1), F32)],
        compiler_params=_params("arbitrary", "arbitrary"),
        name="sb_attn_prompt",
    )(bias, q, kt, vt)


def _sb_sample_body(pt_ref, bias_ref, q_ref, kn_ref, vn_ref, *refs, pages_per_step):
    k_refs = refs[:pages_per_step]
    v_refs = refs[pages_per_step:2 * pages_per_step]
    o_ref = refs[2 * pages_per_step]
    qbd_ref, acc_ref, carry_ref, kpad_ref, vpad_ref = refs[2 * pages_per_step + 1:]
    s = pl.program_id(1)
    nq = kn_ref.shape[-1]
    rows = SB_HEADS * SUBLANES
    upper = _minus_from_key_on(PAGE_SIZE)

    def visit(pages, mask):
        bias = bias_ref[...] * LOG2E
        zs = [_dot(qbd_ref[...], k3.astype(BF16).reshape(D_MODEL, PAGE_SIZE)) + bias
              for k3, _ in pages]
        blocks = [_block_exponents(z, mask, upper) for z in zs]
        weights, carry = _block_weights(blocks, carry_ref[...])
        acc = acc_ref[...]
        for a, (_, v3) in zip(weights, pages):
            a3 = a.reshape(SB_HEADS, SUBLANES, PAGE_SIZE)
            acc = acc + lax.dot_general(a3, v3, (((2,), (2,)), ((0,), (0,))),
                                        preferred_element_type=F32)
        acc_ref[...] = acc
        carry_ref[...] = carry

    @pl.when(s == 0)
    def _():
        qrep = jnp.concatenate([q_ref[...]] * (D_MODEL // LANES), axis=1)
        row_head = lax.broadcasted_iota(jnp.int32, (rows, D_MODEL), 0) // SUBLANES
        col_head = lax.broadcasted_iota(jnp.int32, (rows, D_MODEL), 1) // SB_HEAD_DIM
        qbd_ref[...] = jnp.where(row_head == col_head, qrep, 0.0).astype(BF16)
        acc_ref[...] = jnp.zeros_like(acc_ref)
        carry_ref[...] = jnp.zeros_like(carry_ref)
        kpad_ref[...] = jnp.zeros_like(kpad_ref)
        vpad_ref[...] = jnp.zeros_like(vpad_ref)
        kpad_ref[:, :, 0:nq] = kn_ref[...]
        vpad_ref[:, :, 0:nq] = vn_ref[...]
        key = lax.broadcasted_iota(jnp.int32, (rows, PAGE_SIZE), 1)
        qry = lax.broadcasted_iota(jnp.int32, (rows, PAGE_SIZE), 0) % SUBLANES
        visit([(kpad_ref[...], vpad_ref[...])], key < qry)

    @pl.when(s > 0)
    def _():
        visit([(k_ref[...], v_ref[...]) for k_ref, v_ref in zip(k_refs, v_refs)], None)

    @pl.when(s == pl.num_programs(1) - 1)
    def _():
        o_ref[...] = acc_ref[...]


def _sb_sample(q, kn, vn, cache_k, cache_v, page_table, bias_col, pages_per_step):
    batch = q.shape[0]
    nq = kn.shape[-1]
    n_pages = page_table.shape[1]
    steps = n_pages // pages_per_step
    rows = SB_HEADS * SUBLANES

    def page_spec(i):
        def index(b, s, pt):
            first = jnp.maximum(s - 1, 0) * pages_per_step
            return (pt[b, n_pages - 1 - (first + i)], 0, 0, 0)
        return pl.BlockSpec((None, SB_HEADS, SB_HEAD_DIM, PAGE_SIZE), index)

    per_seq = lambda b, s, pt: (b, 0, 0, 0)
    new_spec = pl.BlockSpec((None, SB_HEADS, SB_HEAD_DIM, nq), per_seq)
    out_spec = pl.BlockSpec((None, SB_HEADS, SUBLANES, SB_HEAD_DIM), per_seq)
    page_specs = [page_spec(i) for i in range(pages_per_step)]
    grid_spec = pltpu.PrefetchScalarGridSpec(
        num_scalar_prefetch=1,
        grid=(batch, steps + 1),
        in_specs=[pl.BlockSpec((rows, 1), lambda b, s, pt: (0, 0)),
                  pl.BlockSpec((None, rows, LANES), lambda b, s, pt: (b, 0, 0)),
                  new_spec, new_spec]
        + page_specs + page_specs,
        out_specs=out_spec,
        scratch_shapes=[pltpu.VMEM((rows, D_MODEL), BF16),
                        pltpu.VMEM((SB_HEADS, SUBLANES, SB_HEAD_DIM), F32),
                        pltpu.VMEM((rows, 1), F32),
                        pltpu.VMEM((SB_HEADS, SB_HEAD_DIM, PAGE_SIZE), F32),
                        pltpu.VMEM((SB_HEADS, SB_HEAD_DIM, PAGE_SIZE), F32)],
    )
    return pl.pallas_call(
        functools.partial(_sb_sample_body, pages_per_step=pages_per_step),
        grid_spec=grid_spec,
        out_shape=jax.ShapeDtypeStruct((batch, SB_HEADS, SUBLANES, SB_HEAD_DIM), F32),
        compiler_params=_params("arbitrary", "arbitrary"),
        name="sb_attn_sample",
    )(page_table, bias_col, q, kn, vn,
      *([cache_k] * pages_per_step), *([cache_v] * pages_per_step))


def _ffn_body(a_ref, wo_ref, x_ref, g_ref, wgu_ref, wd_ref, o_ref, h_ref, *, hidden, chunk):
    x = x_ref[...] + _dot(a_ref[...].astype(BF16), wo_ref[...])
    h_ref[...] = _rms_rows(x, g_ref[...]).astype(BF16)
    o_ref[...] = x

    def step(ci, _):
        off = pl.multiple_of(ci * chunk, chunk)
        off_up = pl.multiple_of(hidden + ci * chunk, chunk)
        gate = _dot(h_ref[...], wgu_ref[:, pl.ds(off, chunk)])
        up = _dot(h_ref[...], wgu_ref[:, pl.ds(off_up, chunk)])
        act = (gate * jax.nn.sigmoid(gate) * up).astype(BF16)
        o_ref[...] += _dot(act, wd_ref[pl.ds(off, chunk), :])
        return 0

    lax.fori_loop(0, hidden // chunk, step, 0, unroll=True)


def _mix_out_ffn(a, wo, x, gain, wgu, wd, tm, chunk=256):
    m = x.shape[0]
    hidden = wd.shape[0]
    row = lambda i: (i, 0)
    return pl.pallas_call(
        functools.partial(_ffn_body, hidden=hidden, chunk=chunk),
        grid=(m // tm,),
        in_specs=[pl.BlockSpec((tm, D_MODEL), row),
                  _resident((D_MODEL, D_MODEL)),
                  pl.BlockSpec((tm, D_MODEL), row),
                  _resident((1, D_MODEL)),
                  _resident((D_MODEL, 2 * hidden)),
                  _resident((hidden, D_MODEL))],
        out_specs=pl.BlockSpec((tm, D_MODEL), row),
        out_shape=jax.ShapeDtypeStruct((m, D_MODEL), F32),
        scratch_shapes=[pltpu.VMEM((tm, D_MODEL), BF16)],
        compiler_params=_params("arbitrary"),
        name="mix_out_ffn",
    )(a, wo, x, gain, wgu, wd)


ML_QK_W = ML_HEADS * ML_QK_DIM
ML_GATES = 2 * ML_HEADS


def _ml_proj_body(x_ref, g_ref, w_ref, wkt_ref, wgt_ref, gb_ref,
                  q_ref, kt_ref, v_ref, og_ref, gr_ref, h_ref):
    h_ref[...] = _rms_rows(x_ref[...], g_ref[...]).astype(BF16)
    q_ref[...] = _dot(h_ref[...], w_ref[:, 0:ML_QK_W]).astype(BF16)
    kt_ref[...] = (_dot_nt(wkt_ref[...], h_ref[...]) * (ML_QK_DIM ** -0.5)).astype(BF16)
    v_ref[...] = _dot(h_ref[...], w_ref[:, ML_QK_W:ML_QK_W + D_MODEL]).astype(BF16)
    og_ref[...] = _dot(h_ref[...], w_ref[:, ML_QK_W + D_MODEL:ML_QK_W + 2 * D_MODEL])
    g = _dot_nt(wgt_ref[...], h_ref[...]) + gb_ref[...]
    g = ML_GATE_CAP * jnp.tanh(g * (1.0 / ML_GATE_CAP))
    is_input_gate = (lax.broadcasted_iota(jnp.int32, g.shape, 0) % 4) < 2
    gr_ref[...] = jnp.where(is_input_gate, g, _log_sigmoid(g))


def _ml_proj(x, gain, w, wkt, wgt, gb, tm):
    m = x.shape[0]
    row = lambda i: (i, 0)
    col = lambda i: (0, i)
    return pl.pallas_call(
        _ml_proj_body,
        grid=(m // tm,),
        in_specs=[pl.BlockSpec((tm, D_MODEL), row),
                  _resident((1, D_MODEL)),
                  _resident((D_MODEL, ML_QK_W + 2 * D_MODEL)),
                  _resident((ML_QK_W, D_MODEL)),
                  _resident((ML_GATES, D_MODEL)),
                  _resident((ML_GATES, 1))],
        out_specs=[pl.BlockSpec((tm, ML_QK_W), row),
                   pl.BlockSpec((ML_QK_W, tm), col),
                   pl.BlockSpec((tm, D_MODEL), row),
                   pl.BlockSpec((tm, D_MODEL), row),
                   pl.BlockSpec((ML_GATES, tm), col)],
        out_shape=[jax.ShapeDtypeStruct((m, ML_QK_W), BF16),
                   jax.ShapeDtypeStruct((ML_QK_W, m), BF16),
                   jax.ShapeDtypeStruct((m, D_MODEL), BF16),
                   jax.ShapeDtypeStruct((m, D_MODEL), F32),
                   jax.ShapeDtypeStruct((ML_GATES, m), F32)],
        scratch_shapes=[pltpu.VMEM((tm, D_MODEL), BF16)],
        compiler_params=_params("arbitrary"),
        name="ml_proj",
    )(x, gain, w, wkt, wgt, gb)


def _mlstm_body(q_ref, kt_ref, v_ref, og_ref, gr_ref, hg_ref, c0_ref, n0_ref, m0_ref,
                y_ref, c_ref, n_ref, m_ref, cn_ref, gate_ref, *, seq, chunk):
    lane = lax.broadcasted_iota(jnp.int32, (1, LANES), 1)
    sub = lax.broadcasted_iota(jnp.int32, (LANES, 1), 0)
    head_lanes = (lane < ML_QK_DIM, lane >= ML_QK_DIM)
    r = lax.broadcasted_iota(jnp.int32, (LANES, LANES), 0)
    c = lax.broadcasted_iota(jnp.int32, (LANES, LANES), 1)
    causal = c <= r
    eye = c == r
    two = lax.broadcasted_iota(jnp.int32, (1, 2), 1)
    pairs = ML_HEADS // 2
    heads = [(p, hd) for p in range(pairs) for hd in range(2)]
    ones = jnp.ones((LANES, LANES), BF16)
    last = slice(LANES - 1, LANES)
    in_chunk = lax.broadcasted_iota(jnp.int32, (ML_HEADS, seq), 1) % chunk

    def chunk_scan(x, op, fill):
        shift = 1
        while shift < chunk:
            x = op(x, jnp.where(in_chunk >= shift, pltpu.roll(x, shift, axis=1), fill))
            shift *= 2
        return x

    m_ref[...] = m0_ref[...]
    for p in range(pairs):
        cn_ref[p, :, 0:ML_V_DIM] = c0_ref[p]
        cn_ref[p, :, ML_V_DIM:] = jnp.broadcast_to(n0_ref[p], (LANES, LANES)).T
    b = chunk_scan(jnp.concatenate([gr_ref[p, 2:4, :] for p in range(pairs)], axis=0),
                   jnp.add, 0.0)
    u = jnp.concatenate([gr_ref[p, 0:2, :] for p in range(pairs)], axis=0) - b
    gate_ref[0] = b
    gate_ref[1] = u
    gate_ref[2] = chunk_scan(u, jnp.maximum, -jnp.inf)

    def step(ci, _):
        rows = pl.ds(pl.multiple_of(ci * chunk, chunk), chunk)
        q = [q_ref[rows, p * LANES:(p + 1) * LANES] for p in range(pairs)]
        kt = [kt_ref[p * LANES:(p + 1) * LANES, rows] for p in range(pairs)]
        b_rows = [gate_ref[0, 2 * p:2 * p + 2, rows] for p in range(pairs)]
        u2 = [gate_ref[1, 2 * p:2 * p + 2, rows] for p in range(pairs)]
        u_max = [gate_ref[2, 2 * p:2 * p + 2, rows] for p in range(pairs)]
        cn = [cn_ref[p] for p in range(pairs)]
        m_prev = {(p, hd): m_ref[p][:, hd:hd + 1] for p, hd in heads}
        vcols = {(p, hd): slice((2 * p + hd) * ML_V_DIM, (2 * p + hd + 1) * ML_V_DIM)
                 for p, hd in heads}
        qm = {(p, hd): jnp.where(head_lanes[hd], q[p], jnp.zeros_like(q[p])) for p, hd in heads}
        qk = {h: _dot(qm[h], kt[h[0]]) for h in heads}
        qcn = {h: _dot(qm[h], cn[h[0]].astype(BF16)) for h in heads}
        big2, mt2 = [], []
        for p in range(pairs):
            prev2 = jnp.concatenate([m_prev[(p, 0)], m_prev[(p, 1)]], axis=0)
            big2.append(jnp.maximum(prev2, u_max[p]))
            mt2.append(b_rows[p] + big2[p])
        cols = [jnp.concatenate([big2[p], mt2[p], big2[p], mt2[p]], axis=0).T
                for p in range(pairs)]
        big_b, s = {}, {}
        for h in heads:
            p, hd = h
            big_b[h] = jnp.broadcast_to(cols[p][:, hd:hd + 1], (LANES, LANES))
            decayed = jnp.where(causal, jnp.exp(u2[p][hd:hd + 1, :] - big_b[h]), 0.0)
            s[h] = (qk[h] * decayed).astype(BF16)
        sv = {h: _dot(s[h], v_ref[rows, vcols[h]]) for h in heads}
        ssum = {h: _dot(s[h], ones) for h in heads}
        hh = {}
        for h in heads:
            p, hd = h
            inter = jnp.exp(m_prev[h] - big_b[h])
            num = inter * qcn[h][:, :ML_V_DIM] + sv[h]
            den = inter * qcn[h][:, ML_V_DIM:] + ssum[h]
            m_t = jnp.broadcast_to(cols[p][:, 2 + hd:3 + hd], (LANES, LANES))
            hh[h] = num / jnp.maximum(jnp.abs(den), jnp.exp(-m_t))
        msq = {h: _dot((hh[h] * hh[h]).astype(BF16), ones) for h in heads}
        for h in heads:
            hn = hh[h] * lax.rsqrt(msq[h] * (1.0 / ML_V_DIM) + RMS_EPS) * hg_ref[:, vcols[h]]
            y_ref[rows, vcols[h]] = (hn * jax.nn.sigmoid(og_ref[rows, vcols[h]])).astype(y_ref.dtype)
        upd, decay, m_new = {}, {}, {}
        first_head = sub < ML_QK_DIM
        for p in range(pairs):
            m_new[p] = mt2[p][:, last]
            b_last = b_rows[p][:, last]
            w2 = jnp.exp(b_last + u2[p] - m_new[p])
            d2 = jnp.exp(b_last + jnp.concatenate([m_prev[(p, 0)], m_prev[(p, 1)]], axis=0)
                         - m_new[p])
            decay[p] = jnp.where(first_head, d2[0:1, :], d2[1:2, :])
            kf = kt[p].astype(F32)
            kwt = jnp.where(first_head, kf * w2[0:1, :], kf * w2[1:2, :]).astype(BF16)
            upd[p] = (_dot(kwt, v_ref[rows, vcols[(p, 0)]]), _dot(kwt, v_ref[rows, vcols[(p, 1)]]),
                      _dot(kwt, ones))
        for p in range(pairs):
            d_c = jnp.where(first_head, upd[p][0], upd[p][1])
            cn_ref[p, :, 0:ML_V_DIM] = decay[p] * cn[p][:, :ML_V_DIM] + d_c
            cn_ref[p, :, ML_V_DIM:] = decay[p] * cn[p][:, ML_V_DIM:] + upd[p][2]
            m_ref[p] = jnp.where(two == 0, m_new[p][0:1, :], m_new[p][1:2, :])
        return 0

    lax.fori_loop(0, seq // chunk, step, 0)
    for p in range(pairs):
        c_ref[p] = cn_ref[p, :, 0:ML_V_DIM]
        n_ref[p] = jnp.sum(jnp.where(eye, cn_ref[p, :, ML_V_DIM:], 0.0), axis=0, keepdims=True)


def _mlstm(q, kt, v, og, gr, head_gain, c0, n0, m0, batch, seq, chunk):
    assert chunk == LANES and seq % chunk == 0
    m = q.shape[0]
    pairs = ML_HEADS // 2
    tok = lambda b: (b, 0)
    st = lambda b: (b, 0, 0, 0)
    states = [pl.BlockSpec((None, pairs, LANES, ML_V_DIM), st),
              pl.BlockSpec((None, pairs, 1, LANES), st),
              pl.BlockSpec((None, pairs, 1, 2), st)]
    return pl.pallas_call(
        functools.partial(_mlstm_body, seq=seq, chunk=chunk),
        grid=(batch,),
        in_specs=[pl.BlockSpec((seq, ML_QK_W), tok),
                  pl.BlockSpec((ML_QK_W, seq), lambda b: (0, b)),
                  pl.BlockSpec((seq, D_MODEL), tok),
                  pl.BlockSpec((seq, D_MODEL), tok),
                  pl.BlockSpec((pairs, None, 4, seq), lambda b: (0, b, 0, 0)),
                  _resident((1, D_MODEL))] + states,
        out_specs=[pl.BlockSpec((seq, D_MODEL), tok)] + states,
        out_shape=[jax.ShapeDtypeStruct((m, D_MODEL), BF16),
                   jax.ShapeDtypeStruct((batch, pairs, LANES, ML_V_DIM), F32),
                   jax.ShapeDtypeStruct((batch, pairs, 1, LANES), F32),
                   jax.ShapeDtypeStruct((batch, pairs, 1, 2), F32)],
        scratch_shapes=[pltpu.VMEM((pairs, LANES, 2 * ML_V_DIM), F32),
                        pltpu.VMEM((3, ML_HEADS, seq), F32)],
        compiler_params=_params("arbitrary"),
        name="mlstm",
    )(q, kt, v, og, gr, head_gain, c0, n0, m0)


SAMPLE_PAD = ML_CHUNK
PAGES_PER_STEP = 16
SB_QUERY_BLOCK = 1024
SB_KEY_BLOCK = 256
SB_PAIRS_PER_STEP = 1


def _pad_tokens(a, batch, seq):
    a = a.reshape(batch, seq, a.shape[-1])
    a = jnp.pad(a, ((0, 0), (0, SAMPLE_PAD - seq), (0, 0)))
    return a.reshape(batch * SAMPLE_PAD, a.shape[-1])


def _heads_last(xt, batch, seq):
    return xt.reshape(batch, SB_HEADS, SB_HEAD_DIM, seq).transpose(0, 3, 1, 2)


def kernel(x_prompt, x_sample, cache_k, cache_v, state_C, state_n, state_m, page_table,
           norm_mix, norm_ffn, w_sb_in, sb_q_gain, sb_k_gain, sb_logit_bias, w_sb_out,
           w_ml_in, ml_gate_bias, ml_head_gain, w_ml_out, ffn_w_gate_up, ffn_w_down):
    bp, tp, _ = x_prompt.shape
    bs, ts, _ = x_sample.shape
    depth = norm_mix.shape[0]
    n_phys = cache_k.shape[1]
    xp = x_prompt.reshape(bp * tp, D_MODEL)
    xs = x_sample.reshape(bs * ts, D_MODEL)
    tm_p, tm_s = 512, bs * ts
    ml_pairs = ML_HEADS // 2
    gate_perm = jnp.array([g for p in range(ml_pairs)
                           for g in (2 * p, 2 * p + 1, ML_HEADS + 2 * p, ML_HEADS + 2 * p + 1)])
    ck = cache_k.transpose(0, 1, 3, 4, 2).reshape(-1, SB_HEADS, SB_HEAD_DIM, PAGE_SIZE)
    cv = cache_v.transpose(0, 1, 3, 4, 2).reshape(-1, SB_HEADS, SB_HEAD_DIM, PAGE_SIZE)

    kp_rows, vp_rows, ks_rows, vs_rows = [], [], [], []
    c_p, n_p, m_p, c_s, n_s, m_s = [], [], [], [], [], []
    for i in range(depth):
        j = i // 2
        gain = norm_mix[i].reshape(1, D_MODEL)
        if i % 2 == 0:
            wq = w_sb_in[j][:, :D_MODEL].astype(BF16)
            wkt = w_sb_in[j][:, D_MODEL:2 * D_MODEL].T.astype(BF16)
            wvt = w_sb_in[j][:, 2 * D_MODEL:].T.astype(BF16)
            w_out = w_sb_out[j].astype(BF16)
            qg = jnp.tile(sb_q_gain[j], SB_HEADS).reshape(1, D_MODEL)
            kg = sb_k_gain[j].reshape(1, SB_HEAD_DIM, 1)
            bias = sb_logit_bias[j]
            q, kt, vt = _sb_qkv(xp, gain, wq, wkt, wvt, qg, kg, tm_p, tp)
            mix_p = _sb_attn(q, kt, vt, bias, bp, tp, SB_QUERY_BLOCK, SB_KEY_BLOCK, SB_PAIRS_PER_STEP)
            kp_rows.append(_heads_last(kt, bp, tp))
            vp_rows.append(_heads_last(vt, bp, tp))
            q, kt, vt = _sb_qkv(xs, gain, wq, wkt, wvt, qg, kg, tm_s, tm_s)
            q8 = q.astype(F32).reshape(bs, ts, SB_HEADS, SB_HEAD_DIM).transpose(0, 2, 1, 3)
            q8 = jnp.pad(q8, ((0, 0), (0, 0), (0, SUBLANES - ts), (0, 0)))
            q8 = jnp.concatenate([q8, q8], axis=-1).reshape(bs, SB_HEADS * SUBLANES, LANES)
            new = lambda xt: xt.reshape(SB_HEADS, SB_HEAD_DIM, bs, ts).transpose(2, 0, 1, 3)
            bias_col = jnp.repeat(bias, SUBLANES).reshape(SB_HEADS * SUBLANES, 1)
            att = _sb_sample(q8, new(kt), new(vt), ck, cv, page_table + j * n_phys,
                             bias_col, PAGES_PER_STEP)
            mix_s = att[:, :, :ts].transpose(0, 2, 1, 3).reshape(bs * ts, D_MODEL)
            ks_rows.append(_heads_last(kt, 1, bs * ts).reshape(bs, ts, SB_HEADS, SB_HEAD_DIM))
            vs_rows.append(_heads_last(vt, 1, bs * ts).reshape(bs, ts, SB_HEADS, SB_HEAD_DIM))
        else:
            n_main = 2 * ML_QK_W + 2 * D_MODEL
            w_main = jnp.concatenate([w_ml_in[j][:, :ML_QK_W], w_ml_in[j][:, 2 * ML_QK_W:n_main]],
                                     axis=1).astype(BF16)
            wkt = w_ml_in[j][:, ML_QK_W:2 * ML_QK_W].T.astype(BF16)
            wgt = w_ml_in[j][:, n_main:][:, gate_perm].T.astype(BF16)
            gb = ml_gate_bias[j][gate_perm].reshape(ML_GATES, 1)
            w_out = w_ml_out[j].astype(BF16)
            hg = ml_head_gain[j].reshape(1, D_MODEL)
            q, kt, v, og, gr = _ml_proj(xp, gain, w_main, wkt, wgt, gb, tm_p)
            zc = jnp.zeros((bp, ml_pairs, LANES, ML_V_DIM), F32)
            zn = jnp.zeros((bp, ml_pairs, 1, LANES), F32)
            zm = jnp.zeros((bp, ml_pairs, 1, 2), F32)
            gr = gr.reshape(ml_pairs, 4, bp, tp).transpose(0, 2, 1, 3)
            mix_p, c1, n1, m1 = _mlstm(q, kt, v, og, gr, hg, zc, zn, zm, bp, tp, ML_CHUNK)
            c_p.append(c1.reshape(bp, ML_HEADS, ML_QK_DIM, ML_V_DIM))
            n_p.append(n1.reshape(bp, ML_HEADS, ML_QK_DIM))
            m_p.append(m1.reshape(bp, ML_HEADS))
            q, kt, v, og, gr = _ml_proj(xs, gain, w_main, wkt, wgt, gb, tm_s)
            kt = jnp.pad(kt.reshape(ML_QK_W, bs, ts), ((0, 0), (0, 0), (0, SAMPLE_PAD - ts)))
            kt = kt.reshape(ML_QK_W, bs * SAMPLE_PAD)
            gr = gr.reshape(ML_GATES, bs, ts)
            pad_i = jnp.full((ML_GATES, bs, SAMPLE_PAD - ts), -jnp.inf, F32)
            pad_f = jnp.zeros((ML_GATES, bs, SAMPLE_PAD - ts), F32)
            is_input_gate = (jnp.arange(ML_GATES) % 4 < 2)[:, None, None]
            gr = jnp.concatenate([gr, jnp.where(is_input_gate, pad_i, pad_f)], axis=-1)
            gr = gr.reshape(ml_pairs, 4, bs, SAMPLE_PAD).transpose(0, 2, 1, 3)
            y, c2, n2, m2 = _mlstm(
                _pad_tokens(q, bs, ts), kt, _pad_tokens(v, bs, ts),
                _pad_tokens(og, bs, ts), gr, hg,
                state_C[j].reshape(bs, ml_pairs, LANES, ML_V_DIM),
                state_n[j].reshape(bs, ml_pairs, 1, LANES),
                state_m[j].reshape(bs, ml_pairs, 1, 2), bs, SAMPLE_PAD, SAMPLE_PAD)
            mix_s = y.reshape(bs, SAMPLE_PAD, D_MODEL)[:, :ts].reshape(bs * ts, D_MODEL)
            c_s.append(c2.reshape(bs, ML_HEADS, ML_QK_DIM, ML_V_DIM))
            n_s.append(n2.reshape(bs, ML_HEADS, ML_QK_DIM))
            m_s.append(m2.reshape(bs, ML_HEADS))
        fg = norm_ffn[i].reshape(1, D_MODEL)
        wgu = ffn_w_gate_up[i].astype(BF16)
        wd = ffn_w_down[i].astype(BF16)
        xp = _mix_out_ffn(mix_p, w_out, xp, fg, wgu, wd, tm_p)
        xs = _mix_out_ffn(mix_s, w_out, xs, fg, wgu, wd, tm_s)

    return (xp.reshape(bp, tp, D_MODEL), xs.reshape(bs, ts, D_MODEL),
            jnp.stack(kp_rows), jnp.stack(vp_rows),
            jnp.stack(c_p), jnp.stack(n_p), jnp.stack(m_p),
            jnp.stack(ks_rows), jnp.stack(vs_rows),
            jnp.stack(c_s), jnp.stack(n_s), jnp.stack(m_s))
```
